```python
import math
import jax, jax.numpy as jnp
from jax import lax
import numpy as np

D_MODEL = 1024
BATCH = 32
SEQ = 2048
DEPTH = 1
DEC_BATCH = 16
DEC_SEQ = 16
PAST_LEN = 2048

CHUNK = 64
A_HEADS = 4
A_DK = 128
A_DV = 128
A_WIDTH = A_HEADS * A_DV
CONV_W = 4
B_HEADS = 8
B_DH = 64
B_WIDTH = B_HEADS * B_DH
BAND_CHUNKS = 8
BAND_PAST = BAND_CHUNKS * CHUNK
REL_MAX = 4 * CHUNK
REL_SIZE = REL_MAX + CHUNK
MIX_WIDTH = A_WIDTH + B_WIDTH
D_FF = 4 * D_MODEL
EPS = 1e-6

QA = 0
KA = QA + A_HEADS * A_DK
VA = KA + A_HEADS * A_DK
GA = VA + A_WIDTH
BA = GA + A_WIDTH
AA = BA + A_HEADS
QB = AA + A_HEADS
KB = QB + B_WIDTH
VB = KB + B_WIDTH
IN_COLS = VB + B_WIDTH
CONV_CH = GA - QA

kernel_name = "hybrid_gdn_chunkband_stream_step"


def rmsnorm(x, g):
    xf = x.astype(jnp.float32)
    y = xf * lax.rsqrt(jnp.mean(xf * xf, axis=-1, keepdims=True) + EPS)
    return (y * g.astype(jnp.float32)).astype(x.dtype)


def l2norm(x):
    xf = x.astype(jnp.float32)
    return xf * lax.rsqrt(jnp.sum(xf * xf, axis=-1, keepdims=True) + EPS)


def modulation(c, w_mod, b_mod):
    m = jax.nn.silu(c) @ w_mod + b_mod
    return jnp.split(m[:, None, :], 6, axis=-1)


def causal_conv_silu(u, left, w):
    full = jnp.concatenate([left.astype(u.dtype), u], axis=1)
    T = u.shape[1]
    out = full[:, 0:T] * w[0]
    for i in range(1, CONV_W):
        out = out + full[:, i:i + T] * w[i]
    return jax.nn.silu(out), full[:, full.shape[1] - (CONV_W - 1):]


def gated_delta_chunked(q, k, v, beta, g, s0):
    B, T, H, DK = q.shape
    DV = v.shape[-1]
    L = min(CHUNK, T)
    n = T // L

    def blk(x):
        x = x.astype(jnp.float32).reshape((B, n, L, H) + x.shape[3:])
        return jnp.moveaxis(x, (1, 3), (0, 2))

    q, k, v, beta, g = blk(q), blk(k), blk(v), blk(beta), blk(g)
    gc = jnp.cumsum(g, axis=-1)
    idx = jnp.arange(L)
    incl = idx[:, None] >= idx[None, :]
    strict = idx[:, None] > idx[None, :]
    decay = jnp.exp(jnp.where(incl, gc[..., :, None] - gc[..., None, :], -jnp.inf))
    kb = k * beta[..., None]
    m = jnp.where(strict, jnp.einsum('nbhid,nbhjd->nbhij', kb, k) * decay, 0.0)
    a = jnp.eye(L, dtype=jnp.float32) + m
    rhs = jnp.concatenate([v * beta[..., None], kb * jnp.exp(gc)[..., None]], axis=-1)
    sol = lax.linalg.triangular_solve(a, rhs, left_side=True, lower=True, unit_diagonal=True)
    u, w = sol[..., :DV], sol[..., DV:]
    qk = jnp.einsum('nbhid,nbhjd->nbhij', q, k) * decay

    def step(s, inp):
        qc, kc, uc, wc, qkc, gcc = inp
        vnew = uc - jnp.einsum('bhld,bhde->bhle', wc, s)
        o = (jnp.einsum('bhld,bhde->bhle', qc * jnp.exp(gcc)[..., None], s)
             + jnp.einsum('bhij,bhje->bhie', qkc, vnew))
        glast = gcc[..., -1]
        s = (s * jnp.exp(glast)[..., None, None]
             + jnp.einsum('bhld,bhle->bhde', kc * jnp.exp(glast[..., None] - gcc)[..., None], vnew))
        return s, o

    s, o = lax.scan(step, s0.astype(jnp.float32), (q, k, u, w, qk, gc))
    o = jnp.moveaxis(o, (0, 2), (1, 3)).reshape(B, T, H, DV)
    return o, s


def mixer_a(proj, conv_left, s0, conv_w, a_log, dt_bias, gdn_norm_g):
    B, T, _ = proj.shape
    qkv, conv_state = causal_conv_silu(proj[..., QA:GA], conv_left, conv_w)
    q = l2norm(qkv[..., QA:KA].reshape(B, T, A_HEADS, A_DK)) * (A_DK ** -0.5)
    k = l2norm(qkv[..., KA:VA].reshape(B, T, A_HEADS, A_DK))
    v = qkv[..., VA:GA].reshape(B, T, A_HEADS, A_DV)
    gate = proj[..., GA:BA].astype(jnp.float32).reshape(B, T, A_HEADS, A_DV)
    beta = jax.nn.sigmoid(proj[..., BA:AA].astype(jnp.float32))
    g = -jnp.exp(a_log.astype(jnp.float32)) * jax.nn.softplus(
        proj[..., AA:QB].astype(jnp.float32) + dt_bias.astype(jnp.float32))
    o, s = gated_delta_chunked(q, k, v, beta, g, s0)
    o = rmsnorm(o, gdn_norm_g) * jax.nn.silu(gate)
    return o.reshape(B, T, A_WIDTH).astype(proj.dtype), conv_state, s


def rel_bias_matrix(rel_bias, qpos, kpos):
    d = jnp.clip(qpos[:, None] - kpos[None, :], -(CHUNK - 1), REL_MAX) + (CHUNK - 1)
    return rel_bias.astype(jnp.float32)[:, d]


def band_attention_prompt(q, k, v, rel_bias):
    B, T, H, DH = q.shape
    n = T // CHUNK
    band = BAND_PAST + CHUNK
    pad = ((0, 0), (BAND_PAST, 0), (0, 0), (0, 0))
    kp, vp = jnp.pad(k, pad), jnp.pad(v, pad)
    j = jnp.arange(band)
    bias = rel_bias_matrix(rel_bias, jnp.arange(CHUNK) + BAND_PAST, j)

    def one_chunk(c):
        start = c * CHUNK
        qc = lax.dynamic_slice_in_dim(q, start, CHUNK, axis=1)
        kc = lax.dynamic_slice_in_dim(kp, start, band, axis=1)
        vc = lax.dynamic_slice_in_dim(vp, start, band, axis=1)
        valid = (start - BAND_PAST + j) >= 0
        s = jnp.einsum('bqhd,bkhd->bhqk', qc, kc).astype(jnp.float32) * (DH ** -0.5) + bias
        p = jax.nn.softmax(jnp.where(valid, s, -jnp.inf), axis=-1).astype(v.dtype)
        return jnp.einsum('bhqk,bkhd->bqhd', p, vc)

    o = lax.map(one_chunk, jnp.arange(n))
    return jnp.moveaxis(o, 0, 1).reshape(B, T, H * DH)


def band_attention_sample(q, k_new, v_new, k_cache, v_cache, rel_bias):
    B, T, H, DH = q.shape
    Lc = k_cache.shape[1]
    kk = jnp.concatenate([k_cache.astype(k_new.dtype), k_new], axis=1)
    vv = jnp.concatenate([v_cache.astype(v_new.dtype), v_new], axis=1)
    bias = rel_bias_matrix(rel_bias, jnp.arange(T) + Lc, jnp.arange(Lc + T))
    s = jnp.einsum('bqhd,bkhd->bhqk', q, kk).astype(jnp.float32) * (DH ** -0.5) + bias
    p = jax.nn.softmax(s, axis=-1).astype(vv.dtype)
    return jnp.einsum('bhqk,bkhd->bqhd', p, vv).reshape(B, T, H * DH)


def trunk_layer(x, c, conv_left, s0, band_cache, w_mod, b_mod, norm1_g, norm2_g, w_in, conv_w,
                a_log, dt_bias, gdn_norm_g, qn_g, kn_g, rel_bias, w_out, w_up, w_down):
    B, T, _ = x.shape
    sh1, sc1, g1, sh2, sc2, g2 = modulation(c, w_mod, b_mod)
    h = rmsnorm(x, norm1_g) * (1 + sc1) + sh1
    proj = h @ w_in
    if conv_left is None:
        conv_left = jnp.zeros((B, CONV_W - 1, CONV_CH), proj.dtype)
        s0 = jnp.zeros((B, A_HEADS, A_DK, A_DV), jnp.float32)
    o_a, conv_state, s = mixer_a(proj, conv_left, s0, conv_w, a_log, dt_bias, gdn_norm_g)
    qb = rmsnorm(proj[..., QB:KB].reshape(B, T, B_HEADS, B_DH), qn_g)
    kb = rmsnorm(proj[..., KB:VB].reshape(B, T, B_HEADS, B_DH), kn_g)
    vb = proj[..., VB:IN_COLS].reshape(B, T, B_HEADS, B_DH)
    if band_cache is None:
        o_b = band_attention_prompt(qb, kb, vb, rel_bias)
        keep = min(BAND_PAST, T)
        k_state, v_state = kb[:, T - keep:], vb[:, T - keep:]
    else:
        o_b = band_attention_sample(qb, kb, vb, band_cache[0], band_cache[1], rel_bias)
        k_state, v_state = kb, vb
    x = x + g1 * (jnp.concatenate([o_a, o_b], axis=-1) @ w_out)
    h = rmsnorm(x, norm2_g) * (1 + sc2) + sh2
    x = x + g2 * (jnp.square(jax.nn.relu(h @ w_up)) @ w_down)
    return x, conv_state, s, k_state, v_state


def setup_inputs(seed: int = 0) -> dict:
    key = jax.random.key(seed)
    ks = jax.random.split(key, 24)
    f32 = jnp.float32

    def nrm(k, shape, s):
        return jax.random.normal(k, shape, f32) * s

    band_len = min(BAND_PAST, PAST_LEN)
    dt = jnp.exp(jax.random.uniform(ks[16], (DEPTH, A_HEADS), f32, math.log(1e-3), math.log(1e-1)))
    return {
        "x_prompt": nrm(ks[0], (BATCH, SEQ, D_MODEL), 1.0),
        "x_sample": nrm(ks[1], (DEC_BATCH, DEC_SEQ, D_MODEL), 1.0),
        "state_conv": nrm(ks[2], (DEPTH, DEC_BATCH, CONV_W - 1, CONV_CH), 1.0),
        "state_gdn": nrm(ks[3], (DEPTH, DEC_BATCH, A_HEADS, A_DK, A_DV), 0.5),
        "cache_k_band": nrm(ks[4], (DEPTH, DEC_BATCH, band_len, B_HEADS, B_DH), 1.0),
        "cache_v_band": nrm(ks[5], (DEPTH, DEC_BATCH, band_len, B_HEADS, B_DH), 1.0),
        "c_prompt": nrm(ks[6], (BATCH, D_MODEL), 1.0),
        "c_sample": nrm(ks[7], (DEC_BATCH, D_MODEL), 1.0),
        "w_mod": nrm(ks[8], (DEPTH, D_MODEL, 6 * D_MODEL), D_MODEL ** -0.5),
        "b_mod": nrm(ks[9], (DEPTH, 6 * D_MODEL), 0.01),
        "norm1_g": 1.0 + nrm(ks[10], (DEPTH, D_MODEL), 0.1),
        "norm2_g": 1.0 + nrm(ks[11], (DEPTH, D_MODEL), 0.1),
        "w_in": nrm(ks[12], (DEPTH, D_MODEL, IN_COLS), D_MODEL ** -0.5),
        "conv_w": nrm(ks[13], (DEPTH, CONV_W, CONV_CH), CONV_W ** -0.5),
        "a_log": jnp.log(jax.random.uniform(ks[14], (DEPTH, A_HEADS), f32, 1.0, 16.0)),
        "dt_bias": dt + jnp.log(-jnp.expm1(-dt)),
        "gdn_norm_g": 1.0 + nrm(ks[15], (DEPTH, A_DV), 0.1),
        "qn_g": 1.0 + nrm(ks[17], (DEPTH, B_DH), 0.1),
        "kn_g": 1.0 + nrm(ks[18], (DEPTH, B_DH), 0.1),
        "rel_bias": nrm(ks[19], (DEPTH, B_HEADS, REL_SIZE), 0.5),
        "w_out": nrm(ks[20], (DEPTH, MIX_WIDTH, D_MODEL), MIX_WIDTH ** -0.5),
        "w_up": nrm(ks[21], (DEPTH, D_MODEL, D_FF), D_MODEL ** -0.5),
        "w_down": nrm(ks[22], (DEPTH, D_FF, D_MODEL), D_FF ** -0.5),
    }


def reference(x_prompt, x_sample, state_conv, state_gdn, cache_k_band, cache_v_band, c_prompt, c_sample,
              w_mod, b_mod, norm1_g, norm2_g, w_in, conv_w, a_log, dt_bias, gdn_norm_g, qn_g, kn_g,
              rel_bias, w_out, w_up, w_down):
    yp, ys = x_prompt, x_sample
    cp_l, gp_l, kp_l, vp_l, cs_l, gs_l, ks_l, vs_l = [], [], [], [], [], [], [], []
    for l in range(DEPTH):
        wl = (w_mod[l], b_mod[l], norm1_g[l], norm2_g[l], w_in[l], conv_w[l], a_log[l], dt_bias[l],
              gdn_norm_g[l], qn_g[l], kn_g[l], rel_bias[l], w_out[l], w_up[l], w_down[l])
        yp, cp, gp, kp, vp = trunk_layer(yp, c_prompt, None, None, None, *wl)
        ys, cs, gs, ksm, vsm = trunk_layer(ys, c_sample, state_conv[l], state_gdn[l],
                                           (cache_k_band[l], cache_v_band[l]), *wl)
        cp_l.append(cp); gp_l.append(gp); kp_l.append(kp); vp_l.append(vp)
        cs_l.append(cs); gs_l.append(gs); ks_l.append(ksm); vs_l.append(vsm)
    conv_prompt = jnp.stack(cp_l)
    gdn_prompt = jnp.stack(gp_l)
    kband_prompt = jnp.stack(kp_l)
    vband_prompt = jnp.stack(vp_l)
    conv_sample = jnp.stack(cs_l)
    gdn_sample = jnp.stack(gs_l)
    knew_sample = jnp.stack(ks_l)
    vnew_sample = jnp.stack(vs_l)
    return (yp, ys, conv_prompt, gdn_prompt, kband_prompt, vband_prompt,
            conv_sample, gdn_sample, knew_sample, vnew_sample)
```

```python
import functools

import jax
import jax.numpy as jnp
from jax import lax
from jax.experimental import pallas as pl
from jax.experimental.pallas import tpu as pltpu

D_MODEL = 1024
CHUNK = 64
A_HEADS = 4
A_DK = 128
A_DV = 128
A_WIDTH = A_HEADS * A_DV
CONV_W = 4
B_HEADS = 8
B_DH = 64
B_WIDTH = B_HEADS * B_DH
BAND_CHUNKS = 8
BAND_PAST = BAND_CHUNKS * CHUNK
BAND = BAND_PAST + CHUNK
REL_MAX = 4 * CHUNK
REL_SIZE = REL_MAX + CHUNK
D_FF = 4 * D_MODEL
EPS = 1e-6

QA = 0
KA = QA + A_HEADS * A_DK
VA = KA + A_HEADS * A_DK
GA = VA + A_WIDTH
BA = GA + A_WIDTH
AA = BA + A_HEADS
QB = AA + A_HEADS
KB = QB + B_WIDTH
VB = KB + B_WIDTH
IN_COLS = VB + B_WIDTH
CONV_CH = GA - QA

LANES = 128
SUBLANES = 8
INV_BLOCK = 16
BIAS_EXT = 640
VMEM_LIMIT = 56 * 1024 * 1024

F32 = jnp.float32
BF16 = jnp.bfloat16
HI = lax.Precision.HIGHEST


def _dot(a, b):
    return jnp.dot(a, b, preferred_element_type=F32)


def _dot_hi(a, b):
    return jnp.dot(a, b, preferred_element_type=F32, precision=HI)


def _dot_nt(a, b):
    return lax.dot_general(a, b, (((1,), (1,)), ((), ())), preferred_element_type=F32)


def _dot_tn(a, b):
    return lax.dot_general(a, b, (((0,), (0,)), ((), ())), preferred_element_type=F32)


def _sigmoid(x):
    return 1.0 / (1.0 + jnp.exp(-x))


def _silu(x):
    return x * _sigmoid(x)


def _softplus(x):
    return jnp.maximum(x, 0.0) + jnp.log(1.0 + jnp.exp(-jnp.abs(x)))


def _const_spec(shape):
    n = len(shape)
    return pl.BlockSpec(shape, lambda *_: (0,) * n)


def _mod_kernel(c_ref, w_ref, b_ref, o_ref):
    c = c_ref[...]
    o_ref[...] = _dot(_silu(c).astype(BF16), w_ref[...].astype(BF16)) + b_ref[...]


def _modulation(c, w_mod, b_mod):
    nb = c.shape[0]
    cols = w_mod.shape[1]
    bn = 512
    return pl.pallas_call(
        _mod_kernel,
        grid=(cols // bn,),
        in_specs=[pl.BlockSpec((nb, D_MODEL), lambda j: (0, 0)),
                  pl.BlockSpec((D_MODEL, bn), lambda j: (0, j)),
                  pl.BlockSpec((1, bn), lambda j: (0, j))],
        out_specs=pl.BlockSpec((nb, bn), lambda j: (0, j)),
        out_shape=jax.ShapeDtypeStruct((nb, cols), F32),
        compiler_params=pltpu.CompilerParams(dimension_semantics=("arbitrary",)),
        name="modulation",
    )(c, w_mod, b_mod.reshape(1, cols))


def _inproj_kernel(x_ref, mod_ref, left_ref, g1_ref, wqkv_ref, wgate_ref, wba_ref, wqb_ref, wkb_ref, wvb_ref,
                   cw_ref, alog_ref, dtb_ref, qng_ref, kng_ref, seg_ref,
                   qa_ref, ka_ref, va_ref, sg_ref, aux_ref, qb_ref, kb_ref, vb_ref, kb32_ref, vb32_ref, conv_ref,
                   buf_ref, *, nb, tt, nt, nkeep):
    t = pl.program_id(1)
    n = nb * tt

    @pl.when(t == 0)
    def _():
        buf_ref[:, 0:SUBLANES, :] = jnp.zeros((nb, SUBLANES, CONV_CH), F32)
        buf_ref[:, SUBLANES - (CONV_W - 1):SUBLANES, :] = left_ref[...]

    x = x_ref[...].reshape(n, D_MODEL)
    mod = mod_ref[...]
    sh1 = mod[:, 0:1, :]
    sc1 = mod[:, 1:2, :]
    ms = jnp.mean(x * x, axis=-1, keepdims=True)
    xn = x * lax.rsqrt(ms + EPS) * g1_ref[...]
    h = (xn.reshape(nb, tt, D_MODEL) * (1.0 + sc1) + sh1).reshape(n, D_MODEL)
    hb = h.astype(BF16)

    pre = _dot(hb, wqkv_ref[...]).reshape(nb, tt, CONV_CH)
    buf_ref[:, SUBLANES:SUBLANES + tt, :] = pre
    cw = cw_ref[...]
    conv = buf_ref[:, 5:5 + tt, :] * cw[0:1, :].reshape(1, 1, CONV_CH)
    conv = conv + buf_ref[:, 6:6 + tt, :] * cw[1:2, :].reshape(1, 1, CONV_CH)
    conv = conv + buf_ref[:, 7:7 + tt, :] * cw[2:3, :].reshape(1, 1, CONV_CH)
    conv = conv + pre * cw[3:4, :].reshape(1, 1, CONV_CH)
    conv_ref[...] = buf_ref[:, tt + 5:tt + SUBLANES, :]
    buf_ref[:, 0:SUBLANES, :] = buf_ref[:, tt:tt + SUBLANES, :]
    act = _silu(conv).reshape(n, CONV_CH)

    for hh in range(A_HEADS):
        lo = hh * A_DK
        qs = act[:, QA + lo:QA + lo + A_DK]
        ks = act[:, KA + lo:KA + lo + A_DK]
        qn = qs * lax.rsqrt(jnp.sum(qs * qs, axis=-1, keepdims=True) + EPS) * (A_DK ** -0.5)
        kn = ks * lax.rsqrt(jnp.sum(ks * ks, axis=-1, keepdims=True) + EPS)
        qa_ref[:, :, lo:lo + A_DK] = qn.astype(BF16).reshape(nb, tt, A_DK)
        ka_ref[:, :, lo:lo + A_DK] = kn.astype(BF16).reshape(nb, tt, A_DK)
    va_ref[...] = act[:, VA:GA].astype(BF16).reshape(nb, tt, A_WIDTH)

    gate = _dot(hb, wgate_ref[...])
    sg_ref[...] = _silu(gate).astype(BF16).reshape(nb, tt, A_WIDTH)

    ba = _dot(hb, wba_ref[...])
    lane = lax.broadcasted_iota(jnp.int32, (n, LANES), 1)
    beta = _sigmoid(ba)
    gdec = -jnp.exp(alog_ref[...]) * _softplus(ba + dtb_ref[...])
    aux = jnp.where(lane < A_HEADS, beta, jnp.where(lane < 2 * A_HEADS, gdec, 0.0))
    aux_ref[...] = aux.reshape(nb, tt, LANES)

    seg = seg_ref[...]
    pq = _dot(hb, wqb_ref[...])
    ssq = _dot((pq * pq).astype(BF16), seg)
    qb = pq * lax.rsqrt(ssq * (1.0 / B_DH) + EPS) * qng_ref[...]
    qb_ref[...] = qb.astype(BF16).reshape(nb, tt, B_WIDTH)
    pk = _dot(hb, wkb_ref[...])
    ssk = _dot((pk * pk).astype(BF16), seg)
    kb = pk * lax.rsqrt(ssk * (1.0 / B_DH) + EPS) * kng_ref[...]
    kb_ref[...] = kb.astype(BF16).reshape(nb, tt, B_WIDTH)
    vb = _dot(hb, wvb_ref[...])
    vb_ref[...] = vb.astype(BF16).reshape(nb, tt, B_WIDTH)

    @pl.when(t >= nt - nkeep)
    def _():
        kb32_ref[...] = kb.reshape(nb, tt, B_WIDTH)
        vb32_ref[...] = vb.reshape(nb, tt, B_WIDTH)


def _inproj(x, mod, left, wts, *, nb, tt):
    bsz, seq, _ = x.shape
    nt = seq // tt
    keep = min(BAND_PAST, seq)
    nkeep = keep // tt
    assert seq % tt == 0 and bsz % nb == 0 and keep % tt == 0 and tt % SUBLANES == 0
    grid = (bsz // nb, nt)
    tok = lambda w: pl.BlockSpec((nb, tt, w), lambda b, t: (b, t, 0))
    per_b = lambda r, w: pl.BlockSpec((nb, r, w), lambda b, t: (b, 0, 0))
    keep_spec = pl.BlockSpec((nb, tt, B_WIDTH), lambda b, t: (b, jnp.maximum(t - (nt - nkeep), 0), 0))
    bf = lambda w: jax.ShapeDtypeStruct((bsz, seq, w), BF16)
    out_shape = (bf(A_WIDTH), bf(A_WIDTH), bf(A_WIDTH), bf(A_WIDTH),
                 jax.ShapeDtypeStruct((bsz, seq, LANES), F32),
                 bf(B_WIDTH), bf(B_WIDTH), bf(B_WIDTH),
                 jax.ShapeDtypeStruct((bsz, keep, B_WIDTH), F32),
                 jax.ShapeDtypeStruct((bsz, keep, B_WIDTH), F32),
                 jax.ShapeDtypeStruct((bsz, CONV_W - 1, CONV_CH), F32))
    out_specs = (tok(A_WIDTH), tok(A_WIDTH), tok(A_WIDTH), tok(A_WIDTH), tok(LANES),
                 tok(B_WIDTH), tok(B_WIDTH), tok(B_WIDTH), keep_spec, keep_spec,
                 per_b(CONV_W - 1, CONV_CH))
    in_specs = [tok(D_MODEL), per_b(6, D_MODEL), per_b(CONV_W - 1, CONV_CH), _const_spec((1, D_MODEL)),
                _const_spec((D_MODEL, CONV_CH)), _const_spec((D_MODEL, A_WIDTH)), _const_spec((D_MODEL, LANES)),
                _const_spec((D_MODEL, B_WIDTH)), _const_spec((D_MODEL, B_WIDTH)), _const_spec((D_MODEL, B_WIDTH)),
                _const_spec((CONV_W, CONV_CH)), _const_spec((1, LANES)), _const_spec((1, LANES)),
                _const_spec((1, B_WIDTH)), _const_spec((1, B_WIDTH)), _const_spec((B_WIDTH, B_WIDTH))]
    return pl.pallas_call(
        functools.partial(_inproj_kernel, nb=nb, tt=tt, nt=nt, nkeep=nkeep),
        grid=grid, in_specs=in_specs, out_specs=out_specs, out_shape=out_shape,
        scratch_shapes=[pltpu.VMEM((nb, tt + SUBLANES, CONV_CH), F32)],
        compiler_params=pltpu.CompilerParams(dimension_semantics=("arbitrary", "arbitrary"),
                                             vmem_limit_bytes=VMEM_LIMIT),
        name="inproj",
    )(x, mod, left, wts["g1"], wts["wqkv"], wts["wgate"], wts["wba"], wts["wqb"], wts["wkb"], wts["wvb"],
      wts["cw"], wts["alog"], wts["dtb"], wts["qng"], wts["kng"], wts["seg"])


def _unit_lower_inverse(m, size):
    ri = lax.broadcasted_iota(jnp.int32, (size, size), 0)
    ci = lax.broadcasted_iota(jnp.int32, (size, size), 1)
    s = min(INV_BLOCK, size)
    sh = s.bit_length() - 1
    nd = jnp.where((ri >> sh) == (ci >> sh), -m, 0.0)
    x = jnp.where(ri == ci, 1.0, 0.0) + nd
    p = nd
    for _ in range(sh - 1):
        p = _dot_hi(p, p)
        x = x + _dot_hi(x, p)
    while s < size:
        same = (ri >> (sh + 1)) == (ci >> (sh + 1))
        low_left = (((ri >> sh) & 1) == 1) & (((ci >> sh) & 1) == 0)
        c = jnp.where(same & low_left, m, 0.0)
        x = x - _dot_hi(_dot_hi(x, c), x)
        s *= 2
        sh += 1
    return x


def _gdn_kernel(qa_ref, ka_ref, va_ref, sg_ref, aux_ref, s0_ref, ng_ref, oa_ref, sout_ref, s_ref, *, L, nc):
    c = pl.program_id(1)

    @pl.when(c == 0)
    def _():
        s_ref[...] = s0_ref[0]

    ri = lax.broadcasted_iota(jnp.int32, (L, L), 0)
    ci = lax.broadcasted_iota(jnp.int32, (L, L), 1)
    incl = ri >= ci
    strict = ri > ci
    aux = aux_ref[0]
    gc = _dot_hi(jnp.where(incl, 1.0, 0.0), aux)
    glast = gc[L - 1:L, :]
    eg = jnp.exp(gc)
    egl = jnp.exp(glast - gc)
    elast = jnp.exp(glast)
    gct = gc.T
    ng = ng_ref[...]

    for hh in range(A_HEADS):
        lo = hh * A_DK
        q = qa_ref[0, :, lo:lo + A_DK].astype(F32)
        k = ka_ref[0, :, lo:lo + A_DK].astype(F32)
        v = va_ref[0, :, lo:lo + A_DV].astype(F32)
        beta_c = aux[:, hh:hh + 1]
        gc_c = gc[:, A_HEADS + hh:A_HEADS + hh + 1]
        eg_c = eg[:, A_HEADS + hh:A_HEADS + hh + 1]
        egl_c = egl[:, A_HEADS + hh:A_HEADS + hh + 1]
        el = elast[:, A_HEADS + hh:A_HEADS + hh + 1]
        gc_r = gct[A_HEADS + hh:A_HEADS + hh + 1, :]
        decay = jnp.exp(jnp.where(incl, gc_c - gc_r, -jnp.inf))
        kb16 = k.astype(BF16)
        qkk = _dot_nt(jnp.concatenate([q, k], axis=0).astype(BF16), kb16)
        qk = qkk[0:L] * decay
        m = jnp.where(strict, qkk[L:2 * L] * beta_c * decay, 0.0)
        tinv = _unit_lower_inverse(m, L)
        rhs = jnp.concatenate([v * beta_c, k * (beta_c * eg_c)], axis=1)
        uw = _dot_hi(tinv, rhs)
        u = uw[:, 0:A_DV]
        w = uw[:, A_DV:A_DV + A_DK]
        s_h = s_ref[hh]
        res = _dot(jnp.concatenate([w, q * eg_c], axis=0).astype(BF16), s_h.astype(BF16))
        vnew = u - res[0:L]
        vnew16 = vnew.astype(BF16)
        o = res[L:2 * L] + _dot(qk.astype(BF16), vnew16)
        s_ref[hh] = s_h * el + _dot_tn((k * egl_c).astype(BF16), vnew16)
        on = o * lax.rsqrt(jnp.mean(o * o, axis=-1, keepdims=True) + EPS) * ng
        sg = sg_ref[0, :, lo:lo + A_DV].astype(F32)
        oa_ref[0, :, lo:lo + A_DV] = (on * sg).astype(BF16)

    @pl.when(c == nc - 1)
    def _():
        sout_ref[0] = s_ref[...]


def _gdn(qa, ka, va, sg, aux, s0, ng, *, L):
    bsz, seq, _ = qa.shape
    nc = seq // L
    assert seq % L == 0
    tok = lambda w: pl.BlockSpec((1, L, w), lambda b, c: (b, c, 0))
    st = pl.BlockSpec((1, A_HEADS, A_DK, A_DV), lambda b, c: (b, 0, 0, 0))
    return pl.pallas_call(
        functools.partial(_gdn_kernel, L=L, nc=nc),
        grid=(bsz, nc),
        in_specs=[tok(A_WIDTH), tok(A_WIDTH), tok(A_WIDTH), tok(A_WIDTH), tok(LANES), st, _const_spec((1, A_DV))],
        out_specs=(tok(A_WIDTH), st),
        out_shape=(jax.ShapeDtypeStruct((bsz, seq, A_WIDTH), BF16),
                   jax.ShapeDtypeStruct((bsz, A_HEADS, A_DK, A_DV), F32)),
        scratch_shapes=[pltpu.VMEM((A_HEADS, A_DK, A_DV), F32)],
        compiler_params=pltpu.CompilerParams(dimension_semantics=("arbitrary", "arbitrary"),
                                             vmem_limit_bytes=VMEM_LIMIT),
        name="gdn",
    )(qa, ka, va, sg, aux, s0, ng)


def _relbias_kernel(rb_ref, o_ref):
    r = lax.broadcasted_iota(jnp.int32, (REL_SIZE, BIAS_EXT), 0)
    p = lax.broadcasted_iota(jnp.int32, (REL_SIZE, BIAS_EXT), 1)
    moff = jnp.where(p < BAND, p, p - BIAS_EXT)
    idx = jnp.clip(BAND_PAST - moff, -(CHUNK - 1), REL_MAX) + (CHUNK - 1)
    base = _dot_hi(rb_ref[...], jnp.where(r == idx, 1.0, 0.0))
    for hh in range(B_HEADS):
        rows = jnp.broadcast_to(base[hh:hh + 1, :], (CHUNK, BIAS_EXT))
        o_ref[hh] = pltpu.roll(rows, 0, 1, stride=1, stride_axis=0)


def _relbias(rel_bias):
    return pl.pallas_call(
        _relbias_kernel,
        in_specs=[pl.BlockSpec((B_HEADS, REL_SIZE), lambda: (0, 0))],
        out_specs=pl.BlockSpec((B_HEADS, CHUNK, BIAS_EXT), lambda: (0, 0, 0)),
        out_shape=jax.ShapeDtypeStruct((B_HEADS, CHUNK, BIAS_EXT), F32),
        name="relbias",
    )(rel_bias)


def _pair_attention(qp, keys, vals, biases, valid):
    lane = lax.broadcasted_iota(jnp.int32, qp.shape, 1)
    outs = []
    for e in range(2):
        mine = (lane >= e * B_DH) & (lane < (e + 1) * B_DH)
        qm = jnp.where(mine, qp, jnp.zeros_like(qp))
        ss = []
        for i, kk in enumerate(keys):
            s = _dot_nt(qm, kk) * (B_DH ** -0.5) + biases[e][i]
            if valid[i] is not None:
                s = jnp.where(valid[i], s, -jnp.inf)
            ss.append(s)
        mx = functools.reduce(jnp.maximum, [jnp.max(s, axis=-1, keepdims=True) for s in ss])
        es = [jnp.exp(s - mx) for s in ss]
        den = functools.reduce(jnp.add, [jnp.sum(e_, axis=-1, keepdims=True) for e_ in es])
        acc = functools.reduce(jnp.add, [_dot((e_ / den).astype(BF16), vv) for e_, vv in zip(es, vals)])
        outs.append(jnp.where(mine, acc, 0.0))
    return outs[0] + outs[1]


def _attn_prompt_kernel(q_ref, k_ref, v_ref, bias_ref, o_ref, kpad_ref, vpad_ref, *, seq):
    c = pl.program_id(1)

    @pl.when(c == 0)
    def _():
        kpad_ref[0:BAND_PAST, :] = jnp.zeros((BAND_PAST, B_WIDTH), BF16)
        vpad_ref[0:BAND_PAST, :] = jnp.zeros((BAND_PAST, B_WIDTH), BF16)
        kpad_ref[BAND_PAST:BAND_PAST + seq, :] = k_ref[0]
        vpad_ref[BAND_PAST:BAND_PAST + seq, :] = v_ref[0]

    start = pl.multiple_of(c * CHUNK, CHUNK)
    j = lax.broadcasted_iota(jnp.int32, (1, BAND), 1)
    valid = (c * CHUNK - BAND_PAST + j) >= 0
    for pp in range(B_HEADS // 2):
        lo = pp * LANES
        qp = q_ref[0, :, lo:lo + LANES]
        kk = kpad_ref[pl.ds(start, BAND), lo:lo + LANES]
        vv = vpad_ref[pl.ds(start, BAND), lo:lo + LANES]
        biases = [[bias_ref[2 * pp + e, :, 0:BAND]] for e in range(2)]
        o_ref[0, :, lo:lo + LANES] = _pair_attention(qp, [kk], [vv], biases, [valid]).astype(BF16)


def _attn_prompt(qb, kb, vb, bias):
    bsz, seq, _ = qb.shape
    nc = seq // CHUNK
    tok = pl.BlockSpec((1, CHUNK, B_WIDTH), lambda b, c: (b, c, 0))
    full = pl.BlockSpec((1, seq, B_WIDTH), lambda b, c: (b, 0, 0))
    return pl.pallas_call(
        functools.partial(_attn_prompt_kernel, seq=seq),
        grid=(bsz, nc),
        in_specs=[tok, full, full, _const_spec((B_HEADS, CHUNK, BIAS_EXT))],
        out_specs=tok,
        out_shape=jax.ShapeDtypeStruct((bsz, seq, B_WIDTH), BF16),
        scratch_shapes=[pltpu.VMEM((BAND_PAST + seq, B_WIDTH), BF16),
                        pltpu.VMEM((BAND_PAST + seq, B_WIDTH), BF16)],
        compiler_params=pltpu.CompilerParams(dimension_semantics=("arbitrary", "arbitrary"),
                                             vmem_limit_bytes=VMEM_LIMIT),
        name="attn_prompt",
    )(qb, kb, vb, bias)


def _attn_sample_kernel(q_ref, kn_ref, vn_ref, kc_ref, vc_ref, bias_ref, o_ref, *, lc, tq):
    kc = kc_ref[0].astype(BF16)
    vc = vc_ref[0].astype(BF16)
    for pp in range(B_HEADS // 2):
        lo = pp * LANES
        qp = q_ref[0, :, lo:lo + LANES]
        keys = [kc[:, lo:lo + LANES], kn_ref[0, :, lo:lo + LANES]]
        vals = [vc[:, lo:lo + LANES], vn_ref[0, :, lo:lo + LANES]]
        off = BAND_PAST - lc
        biases = [[bias_ref[2 * pp + e, 0:tq, off:off + lc], bias_ref[2 * pp + e, 0:tq, BAND_PAST:BAND_PAST + tq]]
                  for e in range(2)]
        o_ref[0, :, lo:lo + LANES] = _pair_attention(qp, keys, vals, biases, [None, None]).astype(BF16)


def _attn_sample(qb, kb, vb, k_cache, v_cache, bias):
    bsz, tq, _ = qb.shape
    lc = k_cache.shape[1]
    assert lc <= BAND_PAST and tq <= CHUNK
    tok = pl.BlockSpec((1, tq, B_WIDTH), lambda b: (b, 0, 0))
    cache = pl.BlockSpec((1, lc, B_WIDTH), lambda b: (b, 0, 0))
    return pl.pallas_call(
        functools.partial(_attn_sample_kernel, lc=lc, tq=tq),
        grid=(bsz,),
        in_specs=[tok, tok, tok, cache, cache, _const_spec((B_HEADS, CHUNK, BIAS_EXT))],
        out_specs=tok,
        out_shape=jax.ShapeDtypeStruct((bsz, tq, B_WIDTH), BF16),
        compiler_params=pltpu.CompilerParams(dimension_semantics=("arbitrary",), vmem_limit_bytes=VMEM_LIMIT),
        name="attn_sample",
    )(qb, kb, vb, k_cache.reshape(bsz, lc, B_WIDTH), v_cache.reshape(bsz, lc, B_WIDTH), bias)


def _outffn_kernel(x_ref, oa_ref, ob_ref, mod_ref, g2_ref, woa_ref, wob_ref, wup_ref, wdn_ref, y_ref, *, nb, tt):
    n = nb * tt
    mod = mod_ref[...]
    gate1 = mod[:, 2:3, :]
    sh2 = mod[:, 3:4, :]
    sc2 = mod[:, 4:5, :]
    gate2 = mod[:, 5:6, :]
    mix = _dot(oa_ref[...].reshape(n, A_WIDTH), woa_ref[...]) + _dot(ob_ref[...].reshape(n, B_WIDTH), wob_ref[...])
    y1 = x_ref[...] + gate1 * mix.reshape(nb, tt, D_MODEL)
    y1f = y1.reshape(n, D_MODEL)
    ms = jnp.mean(y1f * y1f, axis=-1, keepdims=True)
    yn = y1f * lax.rsqrt(ms + EPS) * g2_ref[...]
    h2 = (yn.reshape(nb, tt, D_MODEL) * (1.0 + sc2) + sh2).reshape(n, D_MODEL).astype(BF16)
    acc = jnp.zeros((n, D_MODEL), F32)
    fc = D_MODEL
    for jj in range(D_FF // fc):
        u = _dot(h2, wup_ref[:, jj * fc:(jj + 1) * fc])
        r = jnp.maximum(u, 0.0)
        acc = acc + _dot((r * r).astype(BF16), wdn_ref[jj * fc:(jj + 1) * fc, :])
    y_ref[...] = y1 + gate2 * acc.reshape(nb, tt, D_MODEL)


def _outffn(x, oa, ob, mod, wts, *, nb, tt):
    bsz, seq, _ = x.shape
    assert seq % tt == 0 and bsz % nb == 0
    tok = lambda w: pl.BlockSpec((nb, tt, w), lambda b, t: (b, t, 0))
    once = lambda shape: pl.BlockSpec(shape, lambda b, t: (0,) * len(shape), pipeline_mode=pl.Buffered(1))
    return pl.pallas_call(
        functools.partial(_outffn_kernel, nb=nb, tt=tt),
        grid=(bsz // nb, seq // tt),
        in_specs=[tok(D_MODEL), tok(A_WIDTH), tok(B_WIDTH),
                  pl.BlockSpec((nb, 6, D_MODEL), lambda b, t: (b, 0, 0)), once((1, D_MODEL)),
                  once((A_WIDTH, D_MODEL)), once((B_WIDTH, D_MODEL)), once((D_MODEL, D_FF)), once((D_FF, D_MODEL))],
        out_specs=tok(D_MODEL),
        out_shape=jax.ShapeDtypeStruct((bsz, seq, D_MODEL), F32),
        compiler_params=pltpu.CompilerParams(dimension_semantics=("arbitrary", "arbitrary"),
                                             vmem_limit_bytes=VMEM_LIMIT),
        name="outffn",
    )(x, oa, ob, mod, wts["g2"], wts["woa"], wts["wob"], wts["wup"], wts["wdn"])


def _prep_weights(norm1_g, norm2_g, w_in, conv_w, a_log, dt_bias, gdn_norm_g, qn_g, kn_g, w_out, w_up, w_down):
    pad_ba = jnp.zeros((D_MODEL, LANES - 2 * A_HEADS), F32)
    pad_row = lambda v: jnp.zeros((1, LANES), F32).at[0, A_HEADS:2 * A_HEADS].set(v)
    head = jnp.arange(B_WIDTH) // B_DH
    return dict(
        g1=norm1_g.reshape(1, D_MODEL), g2=norm2_g.reshape(1, D_MODEL),
        wqkv=w_in[:, QA:GA].astype(BF16), wgate=w_in[:, GA:BA].astype(BF16),
        wba=jnp.concatenate([w_in[:, BA:QB], pad_ba], axis=1).astype(BF16),
        wqb=w_in[:, QB:KB].astype(BF16), wkb=w_in[:, KB:VB].astype(BF16), wvb=w_in[:, VB:IN_COLS].astype(BF16),
        cw=conv_w, alog=pad_row(a_log), dtb=pad_row(dt_bias), ng=gdn_norm_g.reshape(1, A_DV),
        qng=jnp.tile(qn_g, B_HEADS).reshape(1, B_WIDTH), kng=jnp.tile(kn_g, B_HEADS).reshape(1, B_WIDTH),
        seg=(head[:, None] == head[None, :]).astype(BF16),
        woa=w_out[:A_WIDTH].astype(BF16), wob=w_out[A_WIDTH:].astype(BF16),
        wup=w_up.astype(BF16), wdn=w_down.astype(BF16))


def _layer(x, mod, left, s0, cache, bias, wts, *, nb, tt, chunk):
    bsz, seq, _ = x.shape
    qa, ka, va, sg, aux, qb, kb, vb, kb32, vb32, conv_state = _inproj(x, mod, left, wts, nb=nb, tt=tt)
    oa, s_new = _gdn(qa, ka, va, sg, aux, s0, wts["ng"], L=chunk)
    if cache is None:
        ob = _attn_prompt(qb, kb, vb, bias)
    else:
        ob = _attn_sample(qb, kb, vb, cache[0], cache[1], bias)
    y = _outffn(x, oa, ob, mod, wts, nb=nb, tt=tt)
    keep = kb32.shape[1]
    return (y, conv_state, s_new, kb32.reshape(bsz, keep, B_HEADS, B_DH), vb32.reshape(bsz, keep, B_HEADS, B_DH))


def kernel(x_prompt, x_sample, state_conv, state_gdn, cache_k_band, cache_v_band, c_prompt, c_sample,
           w_mod, b_mod, norm1_g, norm2_g, w_in, conv_w, a_log, dt_bias, gdn_norm_g, qn_g, kn_g,
           rel_bias, w_out, w_up, w_down):
    depth = w_mod.shape[0]
    bp, tp, _ = x_prompt.shape
    bs, ts, _ = x_sample.shape
    yp, ys = x_prompt, x_sample
    c_all = jnp.concatenate([c_prompt, c_sample], axis=0)
    outs = [[] for _ in range(8)]
    for l in range(depth):
        wts = _prep_weights(norm1_g[l], norm2_g[l], w_in[l], conv_w[l], a_log[l], dt_bias[l], gdn_norm_g[l],
                            qn_g[l], kn_g[l], w_out[l], w_up[l], w_down[l])
        mod = _modulation(c_all, w_mod[l], b_mod[l]).reshape(bp + bs, 6, D_MODEL)
        bias = _relbias(rel_bias[l])
        yp, cp, gp, kp, vp = _layer(
            yp, mod[:bp], jnp.zeros((bp, CONV_W - 1, CONV_CH), F32), jnp.zeros((bp, A_HEADS, A_DK, A_DV), F32),
            None, bias, wts, nb=1, tt=min(512, tp), chunk=min(CHUNK, tp))
        ys, cs, gs, ks, vs = _layer(
            ys, mod[bp:], state_conv[l], state_gdn[l], (cache_k_band[l], cache_v_band[l]), bias, wts,
            nb=bs, tt=ts, chunk=min(CHUNK, ts))
        for acc, val in zip(outs, (cp, gp, kp, vp, cs, gs, ks, vs)):
            acc.append(val)
    return (yp, ys) + tuple(jnp.stack(o) for o in outs)
```

```python
import functools

import jax
import jax.numpy as jnp
from jax import lax
from jax.experimental import pallas as pl
from jax.experimental.pallas import tpu as pltpu

D_MODEL = 1024
CHUNK = 64
A_HEADS = 4
A_DK = 128
A_DV = 128
A_WIDTH = A_HEADS * A_DV
CONV_W = 4
B_HEADS = 8
B_DH = 64
B_WIDTH = B_HEADS * B_DH
BAND_CHUNKS = 8
BAND_PAST = BAND_CHUNKS * CHUNK
BAND = BAND_PAST + CHUNK
REL_MAX = 4 * CHUNK
REL_SIZE = REL_MAX + CHUNK
D_FF = 4 * D_MODEL
EPS = 1e-6

QA = 0
KA = QA + A_HEADS * A_DK
VA = KA + A_HEADS * A_DK
GA = VA + A_WIDTH
BA = GA + A_WIDTH
AA = BA + A_HEADS
QB = AA + A_HEADS
KB = QB + B_WIDTH
VB = KB + B_WIDTH
IN_COLS = VB + B_WIDTH
CONV_CH = GA - QA

LANES = 128
SUBLANES = 8
INV_BLOCK = 16
QBLK = 4 * CHUNK
WIN = BAND_PAST + QBLK
BIAS_EXT = 1024
CHUNK_SHIFT = CHUNK.bit_length() - 1
B_DH_SHIFT = B_DH.bit_length() - 1
LOG2E = 1.4426950408889634
MASKED = -1e30
VMEM_LIMIT = 56 * 1024 * 1024

F32 = jnp.float32
BF16 = jnp.bfloat16
HI = lax.Precision.HIGHEST


def _dot(a, b):
    return jnp.dot(a, b, preferred_element_type=F32)


def _dot_hi(a, b):
    return jnp.dot(a, b, preferred_element_type=F32, precision=HI)


def _dot_nt(a, b):
    return lax.dot_general(a, b, (((1,), (1,)), ((), ())), preferred_element_type=F32)


def _dot_tn(a, b):
    return lax.dot_general(a, b, (((0,), (0,)), ((), ())), preferred_element_type=F32)


def _sigmoid(x):
    return 1.0 / (1.0 + jnp.exp(-x))


def _silu(x):
    return x * _sigmoid(x)


def _softplus(x):
    return jnp.maximum(x, 0.0) + jnp.log(1.0 + jnp.exp(-jnp.abs(x)))


def _const_spec(shape):
    n = len(shape)
    return pl.BlockSpec(shape, lambda *_: (0,) * n)


def _mod_kernel(c_ref, w_ref, b_ref, o_ref):
    c = c_ref[...]
    o_ref[...] = _dot(_silu(c).astype(BF16), w_ref[...].astype(BF16)) + b_ref[...]


def _modulation(c, w_mod, b_mod):
    nb = c.shape[0]
    cols = w_mod.shape[1]
    bn = 512
    return pl.pallas_call(
        _mod_kernel,
        grid=(cols // bn,),
        in_specs=[pl.BlockSpec((nb, D_MODEL), lambda j: (0, 0)),
                  pl.BlockSpec((D_MODEL, bn), lambda j: (0, j)),
                  pl.BlockSpec((1, bn), lambda j: (0, j))],
        out_specs=pl.BlockSpec((nb, bn), lambda j: (0, j)),
        out_shape=jax.ShapeDtypeStruct((nb, cols), F32),
        compiler_params=pltpu.CompilerParams(dimension_semantics=("arbitrary",)),
        name="modulation",
    )(c, w_mod, b_mod.reshape(1, cols))


def _inproj_kernel(x_ref, mod_ref, left_ref, g1_ref, wqkv_ref, wgate_ref, wba_ref, wqb_ref, wkb_ref, wvb_ref,
                   cw_ref, alog_ref, dtb_ref, qng_ref, kng_ref, seg_ref,
                   qa_ref, ka_ref, va_ref, sg_ref, aux_ref, qb_ref, kb_ref, vb_ref, kb32_ref, vb32_ref, conv_ref,
                   buf_ref, *, nb, tt, nt, nkeep):
    t = pl.program_id(1)
    n = nb * tt

    @pl.when(t == 0)
    def _():
        buf_ref[:, 0:SUBLANES, :] = jnp.zeros((nb, SUBLANES, CONV_CH), F32)
        buf_ref[:, SUBLANES - (CONV_W - 1):SUBLANES, :] = left_ref[...]

    x = x_ref[...].reshape(n, D_MODEL)
    mod = mod_ref[...]
    sh1 = mod[:, 0:1, :]
    sc1 = mod[:, 1:2, :]
    ms = jnp.mean(x * x, axis=-1, keepdims=True)
    xn = x * lax.rsqrt(ms + EPS) * g1_ref[...]
    h = (xn.reshape(nb, tt, D_MODEL) * (1.0 + sc1) + sh1).reshape(n, D_MODEL)
    hb = h.astype(BF16)

    pre = _dot(hb, wqkv_ref[...]).reshape(nb, tt, CONV_CH)
    buf_ref[:, SUBLANES:SUBLANES + tt, :] = pre
    cw = cw_ref[...]
    conv = buf_ref[:, 5:5 + tt, :] * cw[0:1, :].reshape(1, 1, CONV_CH)
    conv = conv + buf_ref[:, 6:6 + tt, :] * cw[1:2, :].reshape(1, 1, CONV_CH)
    conv = conv + buf_ref[:, 7:7 + tt, :] * cw[2:3, :].reshape(1, 1, CONV_CH)
    conv = conv + pre * cw[3:4, :].reshape(1, 1, CONV_CH)
    conv_ref[...] = buf_ref[:, tt + 5:tt + SUBLANES, :]
    buf_ref[:, 0:SUBLANES, :] = buf_ref[:, tt:tt + SUBLANES, :]
    act = _silu(conv).reshape(n, CONV_CH)

    for hh in range(A_HEADS):
        lo = hh * A_DK
        qs = act[:, QA + lo:QA + lo + A_DK]
        ks = act[:, KA + lo:KA + lo + A_DK]
        qn = qs * lax.rsqrt(jnp.sum(qs * qs, axis=-1, keepdims=True) + EPS) * (A_DK ** -0.5)
        kn = ks * lax.rsqrt(jnp.sum(ks * ks, axis=-1, keepdims=True) + EPS)
        qa_ref[:, :, lo:lo + A_DK] = qn.astype(BF16).reshape(nb, tt, A_DK)
        ka_ref[:, :, lo:lo + A_DK] = kn.astype(BF16).reshape(nb, tt, A_DK)
    va_ref[...] = act[:, VA:GA].astype(BF16).reshape(nb, tt, A_WIDTH)

    gate = _dot(hb, wgate_ref[...])
    sg_ref[...] = _silu(gate).astype(BF16).reshape(nb, tt, A_WIDTH)

    ba = _dot(hb, wba_ref[...])
    lane = lax.broadcasted_iota(jnp.int32, (n, LANES), 1)
    beta = _sigmoid(ba)
    gdec = -jnp.exp(alog_ref[...]) * _softplus(ba + dtb_ref[...])
    aux = jnp.where(lane < A_HEADS, beta, jnp.where(lane < 2 * A_HEADS, gdec, 0.0))
    aux_ref[...] = aux.reshape(nb, tt, LANES)

    seg = seg_ref[...]
    pq = _dot(hb, wqb_ref[...])
    ssq = _dot((pq * pq).astype(BF16), seg)
    qb = pq * lax.rsqrt(ssq * (1.0 / B_DH) + EPS) * qng_ref[...] * (B_DH ** -0.5 * LOG2E)
    qb_ref[...] = qb.astype(BF16).reshape(nb, tt, B_WIDTH)
    pk = _dot(hb, wkb_ref[...])
    ssk = _dot((pk * pk).astype(BF16), seg)
    kb = pk * lax.rsqrt(ssk * (1.0 / B_DH) + EPS) * kng_ref[...]
    kb_ref[...] = kb.astype(BF16).reshape(nb, tt, B_WIDTH)
    vb = _dot(hb, wvb_ref[...])
    vb_ref[...] = vb.astype(BF16).reshape(nb, tt, B_WIDTH)

    @pl.when(t >= nt - nkeep)
    def _():
        kb32_ref[...] = kb.reshape(nb, tt, B_WIDTH)
        vb32_ref[...] = vb.reshape(nb, tt, B_WIDTH)


def _inproj(x, mod, left, wts, *, nb, tt):
    bsz, seq, _ = x.shape
    nt = seq // tt
    keep = min(BAND_PAST, seq)
    nkeep = keep // tt
    assert seq % tt == 0 and bsz % nb == 0 and keep % tt == 0 and tt % SUBLANES == 0
    grid = (bsz // nb, nt)
    tok = lambda w: pl.BlockSpec((nb, tt, w), lambda b, t: (b, t, 0))
    per_b = lambda r, w: pl.BlockSpec((nb, r, w), lambda b, t: (b, 0, 0))
    keep_spec = pl.BlockSpec((nb, tt, B_WIDTH), lambda b, t: (b, jnp.maximum(t - (nt - nkeep), 0), 0))
    bf = lambda w: jax.ShapeDtypeStruct((bsz, seq, w), BF16)
    out_shape = (bf(A_WIDTH), bf(A_WIDTH), bf(A_WIDTH), bf(A_WIDTH),
                 jax.ShapeDtypeStruct((bsz, seq, LANES), F32),
                 bf(B_WIDTH), bf(B_WIDTH), bf(B_WIDTH),
                 jax.ShapeDtypeStruct((bsz, keep, B_WIDTH), F32),
                 jax.ShapeDtypeStruct((bsz, keep, B_WIDTH), F32),
                 jax.ShapeDtypeStruct((bsz, CONV_W - 1, CONV_CH), F32))
    out_specs = (tok(A_WIDTH), tok(A_WIDTH), tok(A_WIDTH), tok(A_WIDTH), tok(LANES),
                 tok(B_WIDTH), tok(B_WIDTH), tok(B_WIDTH), keep_spec, keep_spec,
                 per_b(CONV_W - 1, CONV_CH))
    in_specs = [tok(D_MODEL), per_b(6, D_MODEL), per_b(CONV_W - 1, CONV_CH), _const_spec((1, D_MODEL)),
                _const_spec((D_MODEL, CONV_CH)), _const_spec((D_MODEL, A_WIDTH)), _const_spec((D_MODEL, LANES)),
                _const_spec((D_MODEL, B_WIDTH)), _const_spec((D_MODEL, B_WIDTH)), _const_spec((D_MODEL, B_WIDTH)),
                _const_spec((CONV_W, CONV_CH)), _const_spec((1, LANES)), _const_spec((1, LANES)),
                _const_spec((1, B_WIDTH)), _const_spec((1, B_WIDTH)), _const_spec((B_WIDTH, B_WIDTH))]
    return pl.pallas_call(
        functools.partial(_inproj_kernel, nb=nb, tt=tt, nt=nt, nkeep=nkeep),
        grid=grid, in_specs=in_specs, out_specs=out_specs, out_shape=out_shape,
        scratch_shapes=[pltpu.VMEM((nb, tt + SUBLANES, CONV_CH), F32)],
        compiler_params=pltpu.CompilerParams(dimension_semantics=("arbitrary", "arbitrary"),
                                             vmem_limit_bytes=VMEM_LIMIT),
        name="inproj",
    )(x, mod, left, wts["g1"], wts["wqkv"], wts["wgate"], wts["wba"], wts["wqb"], wts["wkb"], wts["wvb"],
      wts["cw"], wts["alog"], wts["dtb"], wts["qng"], wts["kng"], wts["seg"])


def _unit_lower_inverse(m, size):
    ri = lax.broadcasted_iota(jnp.int32, (size, size), 0)
    ci = lax.broadcasted_iota(jnp.int32, (size, size), 1)
    s = min(INV_BLOCK, size)
    sh = s.bit_length() - 1
    nd = jnp.where((ri >> sh) == (ci >> sh), -m, 0.0)
    x = jnp.where(ri == ci, 1.0, 0.0) + nd
    p = nd
    for _ in range(sh - 1):
        p = _dot_hi(p, p)
        x = x + _dot_hi(x, p)
    while s < size:
        same = (ri >> (sh + 1)) == (ci >> (sh + 1))
        low_left = (((ri >> sh) & 1) == 1) & (((ci >> sh) & 1) == 0)
        c = jnp.where(same & low_left, m, 0.0)
        x = x - _dot_hi(_dot_hi(x, c), x)
        s *= 2
        sh += 1
    return x


def _gdn_kernel(qa_ref, ka_ref, va_ref, sg_ref, aux_ref, s0_ref, ng_ref, oa_ref, sout_ref, s_ref, *, L, nc):
    c = pl.program_id(1)

    @pl.when(c == 0)
    def _():
        s_ref[...] = s0_ref[0]

    ri = lax.broadcasted_iota(jnp.int32, (L, L), 0)
    ci = lax.broadcasted_iota(jnp.int32, (L, L), 1)
    incl = ri >= ci
    strict = ri > ci
    aux = aux_ref[0]
    gc = _dot_hi(jnp.where(incl, 1.0, 0.0), aux)
    glast = gc[L - 1:L, :]
    eg = jnp.exp(gc)
    egl = jnp.exp(glast - gc)
    elast = jnp.exp(glast)
    gct = gc.T
    ng = ng_ref[...]

    for hh in range(A_HEADS):
        lo = hh * A_DK
        q = qa_ref[0, :, lo:lo + A_DK].astype(F32)
        k = ka_ref[0, :, lo:lo + A_DK].astype(F32)
        v = va_ref[0, :, lo:lo + A_DV].astype(F32)
        beta_c = aux[:, hh:hh + 1]
        gc_c = gc[:, A_HEADS + hh:A_HEADS + hh + 1]
        eg_c = eg[:, A_HEADS + hh:A_HEADS + hh + 1]
        egl_c = egl[:, A_HEADS + hh:A_HEADS + hh + 1]
        el = elast[:, A_HEADS + hh:A_HEADS + hh + 1]
        gc_r = gct[A_HEADS + hh:A_HEADS + hh + 1, :]
        decay = jnp.exp(jnp.where(incl, gc_c - gc_r, -jnp.inf))
        kb16 = k.astype(BF16)
        qkk = _dot_nt(jnp.concatenate([q, k], axis=0).astype(BF16), kb16)
        qk = qkk[0:L] * decay
        m = jnp.where(strict, qkk[L:2 * L] * beta_c * decay, 0.0)
        tinv = _unit_lower_inverse(m, L)
        rhs = jnp.concatenate([v * beta_c, k * (beta_c * eg_c)], axis=1)
        uw = _dot_hi(tinv, rhs)
        u = uw[:, 0:A_DV]
        w = uw[:, A_DV:A_DV + A_DK]
        s_h = s_ref[hh]
        res = _dot(jnp.concatenate([w, q * eg_c], axis=0).astype(BF16), s_h.astype(BF16))
        vnew = u - res[0:L]
        vnew16 = vnew.astype(BF16)
        o = res[L:2 * L] + _dot(qk.astype(BF16), vnew16)
        s_ref[hh] = s_h * el + _dot_tn((k * egl_c).astype(BF16), vnew16)
        on = o * lax.rsqrt(jnp.mean(o * o, axis=-1, keepdims=True) + EPS) * ng
        sg = sg_ref[0, :, lo:lo + A_DV].astype(F32)
        oa_ref[0, :, lo:lo + A_DV] = (on * sg).astype(BF16)

    @pl.when(c == nc - 1)
    def _():
        sout_ref[0] = s_ref[...]


def _gdn(qa, ka, va, sg, aux, s0, ng, *, L):
    bsz, seq, _ = qa.shape
    nc = seq // L
    assert seq % L == 0
    tok = lambda w: pl.BlockSpec((1, L, w), lambda b, c: (b, c, 0))
    st = pl.BlockSpec((1, A_HEADS, A_DK, A_DV), lambda b, c: (b, 0, 0, 0))
    return pl.pallas_call(
        functools.partial(_gdn_kernel, L=L, nc=nc),
        grid=(bsz, nc),
        in_specs=[tok(A_WIDTH), tok(A_WIDTH), tok(A_WIDTH), tok(A_WIDTH), tok(LANES), st, _const_spec((1, A_DV))],
        out_specs=(tok(A_WIDTH), st),
        out_shape=(jax.ShapeDtypeStruct((bsz, seq, A_WIDTH), BF16),
                   jax.ShapeDtypeStruct((bsz, A_HEADS, A_DK, A_DV), F32)),
        scratch_shapes=[pltpu.VMEM((A_HEADS, A_DK, A_DV), F32)],
        compiler_params=pltpu.CompilerParams(dimension_semantics=("arbitrary", "arbitrary"),
                                             vmem_limit_bytes=VMEM_LIMIT),
        name="gdn",
    )(qa, ka, va, sg, aux, s0, ng)


def _relbias_kernel(rb_ref, o_ref):
    r = lax.broadcasted_iota(jnp.int32, (REL_SIZE, BIAS_EXT), 0)
    p = lax.broadcasted_iota(jnp.int32, (REL_SIZE, BIAS_EXT), 1)
    moff = jnp.where(p < WIN, p, p - BIAS_EXT)
    idx = jnp.clip(BAND_PAST - moff, -(CHUNK - 1), REL_MAX) + (CHUNK - 1)
    base = _dot_hi(rb_ref[...], jnp.where(r == idx, 1.0, 0.0)) * LOG2E
    qi = lax.broadcasted_iota(jnp.int32, (QBLK, WIN), 0) >> CHUNK_SHIFT
    kj = lax.broadcasted_iota(jnp.int32, (QBLK, WIN), 1) >> CHUNK_SHIFT
    in_band = (kj >= qi) & (kj <= qi + BAND_CHUNKS)
    for hh in range(B_HEADS):
        rows = jnp.broadcast_to(base[hh:hh + 1, :], (QBLK, BIAS_EXT))
        toep = pltpu.roll(rows, 0, 1, stride=1, stride_axis=0)
        o_ref[hh] = jnp.where(in_band, toep[:, 0:WIN], MASKED)


def _relbias(rel_bias):
    return pl.pallas_call(
        _relbias_kernel,
        in_specs=[pl.BlockSpec((B_HEADS, REL_SIZE), lambda: (0, 0))],
        out_specs=pl.BlockSpec((B_HEADS, QBLK, WIN), lambda: (0, 0, 0)),
        out_shape=jax.ShapeDtypeStruct((B_HEADS, QBLK, WIN), F32),
        compiler_params=pltpu.CompilerParams(vmem_limit_bytes=VMEM_LIMIT),
        name="relbias",
    )(rel_bias)


def _attn_prompt_kernel(q_ref, k_ref, v_ref, bias_ref, o_ref, kaug_ref, vaug_ref, *, seq):
    c = pl.program_id(1)

    @pl.when(c == 0)
    def _():
        lane = lax.broadcasted_iota(jnp.int32, (seq, B_WIDTH), 1)
        even = ((lane >> B_DH_SHIFT) & 1) == 0
        k = k_ref[0]
        v = v_ref[0]
        for e in range(2):
            mine = even if e == 0 else jnp.logical_not(even)
            kaug_ref[e, 0:BAND_PAST, :] = jnp.zeros((BAND_PAST, B_WIDTH), BF16)
            vaug_ref[e, 0:BAND_PAST, :] = jnp.zeros((BAND_PAST, B_WIDTH), BF16)
            kaug_ref[e, BAND_PAST:BAND_PAST + seq, :] = jnp.where(mine, k, jnp.zeros_like(k))
            vaug_ref[e, BAND_PAST:BAND_PAST + seq, :] = jnp.where(mine, v, jnp.ones_like(v))

    start = pl.multiple_of(c * QBLK, QBLK)
    lane = lax.broadcasted_iota(jnp.int32, (QBLK, LANES), 1)
    for pp in range(B_HEADS // 2):
        lo = pp * LANES
        qp = q_ref[0, :, lo:lo + LANES]
        res = []
        for e in range(2):
            kk = kaug_ref[e, pl.ds(start, WIN), lo:lo + LANES]
            vv = vaug_ref[e, pl.ds(start, WIN), lo:lo + LANES]
            s = _dot_nt(qp, kk) + bias_ref[2 * pp + e]
            pr = jnp.exp2(s - jnp.max(s, axis=-1, keepdims=True)).astype(BF16)
            res.append(_dot(pr, vv))
        num = jnp.where(lane < B_DH, res[0], res[1])
        den = pltpu.roll(jnp.where(lane < B_DH, res[1], res[0]), B_DH, 1)
        o_ref[0, :, lo:lo + LANES] = (num / den).astype(BF16)


def _attn_prompt(qb, kb, vb, bias):
    bsz, seq, _ = qb.shape
    assert seq % QBLK == 0
    tok = pl.BlockSpec((1, QBLK, B_WIDTH), lambda b, c: (b, c, 0))
    full = pl.BlockSpec((1, seq, B_WIDTH), lambda b, c: (b, 0, 0))
    bias_spec = pl.BlockSpec((B_HEADS, QBLK, WIN), lambda b, c: (0, 0, 0), pipeline_mode=pl.Buffered(1))
    return pl.pallas_call(
        functools.partial(_attn_prompt_kernel, seq=seq),
        grid=(bsz, seq // QBLK),
        in_specs=[tok, full, full, bias_spec],
        out_specs=tok,
        out_shape=jax.ShapeDtypeStruct((bsz, seq, B_WIDTH), BF16),
        scratch_shapes=[pltpu.VMEM((2, BAND_PAST + seq, B_WIDTH), BF16),
                        pltpu.VMEM((2, BAND_PAST + seq, B_WIDTH), BF16)],
        compiler_params=pltpu.CompilerParams(dimension_semantics=("arbitrary", "arbitrary"),
                                             vmem_limit_bytes=VMEM_LIMIT),
        name="attn_prompt",
    )(qb, kb, vb, bias)


def _attn_sample_kernel(q_ref, kn_ref, vn_ref, kc_ref, vc_ref, bias_ref, o_ref, *, lc, tq):
    kc = kc_ref[0].astype(BF16)
    vc = vc_ref[0].astype(BF16)
    off = BAND_PAST - lc
    lane = lax.broadcasted_iota(jnp.int32, (tq, LANES), 1)
    for pp in range(B_HEADS // 2):
        lo = pp * LANES
        qp = q_ref[0, :, lo:lo + LANES]
        keys = [kc[:, lo:lo + LANES], kn_ref[0, :, lo:lo + LANES]]
        vals = [vc[:, lo:lo + LANES], vn_ref[0, :, lo:lo + LANES]]
        out = jnp.zeros((tq, LANES), F32)
        for e in range(2):
            hh = 2 * pp + e
            mine = (lane >= e * B_DH) & (lane < (e + 1) * B_DH)
            qm = jnp.where(mine, qp, jnp.zeros_like(qp))
            ss = [_dot_nt(qm, keys[0]) + bias_ref[hh, 0:tq, off:off + lc],
                  _dot_nt(qm, keys[1]) + bias_ref[hh, 0:tq, BAND_PAST:BAND_PAST + tq]]
            mx = jnp.maximum(jnp.max(ss[0], axis=-1, keepdims=True), jnp.max(ss[1], axis=-1, keepdims=True))
            es = [jnp.exp2(s - mx) for s in ss]
            den = jnp.sum(es[0], axis=-1, keepdims=True) + jnp.sum(es[1], axis=-1, keepdims=True)
            acc = _dot((es[0] / den).astype(BF16), vals[0]) + _dot((es[1] / den).astype(BF16), vals[1])
            out = out + jnp.where(mine, acc, 0.0)
        o_ref[0, :, lo:lo + LANES] = out.astype(BF16)


def _attn_sample(qb, kb, vb, k_cache, v_cache, bias):
    bsz, tq, _ = qb.shape
    lc = k_cache.shape[1]
    assert lc <= BAND_PAST and tq <= CHUNK
    tok = pl.BlockSpec((1, tq, B_WIDTH), lambda b: (b, 0, 0))
    cache = pl.BlockSpec((1, lc, B_WIDTH), lambda b: (b, 0, 0))
    bias_spec = pl.BlockSpec((B_HEADS, QBLK, WIN), lambda b: (0, 0, 0), pipeline_mode=pl.Buffered(1))
    return pl.pallas_call(
        functools.partial(_attn_sample_kernel, lc=lc, tq=tq),
        grid=(bsz,),
        in_specs=[tok, tok, tok, cache, cache, bias_spec],
        out_specs=tok,
        out_shape=jax.ShapeDtypeStruct((bsz, tq, B_WIDTH), BF16),
        compiler_params=pltpu.CompilerParams(dimension_semantics=("arbitrary",), vmem_limit_bytes=VMEM_LIMIT),
        name="attn_sample",
    )(qb, kb, vb, k_cache.reshape(bsz, lc, B_WIDTH), v_cache.reshape(bsz, lc, B_WIDTH), bias)


def _outffn_kernel(x_ref, oa_ref, ob_ref, mod_ref, g2_ref, woa_ref, wob_ref, wup_ref, wdn_ref, y_ref, *, nb, tt):
    n = nb * tt
    mod = mod_ref[...]
    gate1 = mod[:, 2:3, :]
    sh2 = mod[:, 3:4, :]
    sc2 = mod[:, 4:5, :]
    gate2 = mod[:, 5:6, :]
    mix = _dot(oa_ref[...].reshape(n, A_WIDTH), woa_ref[...]) + _dot(ob_ref[...].reshape(n, B_WIDTH), wob_ref[...])
    y1 = x_ref[...] + gate1 * mix.reshape(nb, tt, D_MODEL)
    y1f = y1.reshape(n, D_MODEL)
    ms = jnp.mean(y1f * y1f, axis=-1, keepdims=True)
    yn = y1f * lax.rsqrt(ms + EPS) * g2_ref[...]
    h2 = (yn.reshape(nb, tt, D_MODEL) * (1.0 + sc2) + sh2).reshape(n, D_MODEL).astype(BF16)
    acc = jnp.zeros((n, D_MODEL), F32)
    fc = D_MODEL
    for jj in range(D_FF // fc):
        u = _dot(h2, wup_ref[:, jj * fc:(jj + 1) * fc])
        r = jnp.maximum(u, 0.0)
        acc = acc + _dot((r * r).astype(BF16), wdn_ref[jj * fc:(jj + 1) * fc, :])
    y_ref[...] = y1 + gate2 * acc.reshape(nb, tt, D_MODEL)


def _outffn(x, oa, ob, mod, wts, *, nb, tt):
    bsz, seq, _ = x.shape
    assert seq % tt == 0 and bsz % nb == 0
    tok = lambda w: pl.BlockSpec((nb, tt, w), lambda b, t: (b, t, 0))
    once = lambda shape: pl.BlockSpec(shape, lambda b, t: (0,) * len(shape), pipeline_mode=pl.Buffered(1))
    return pl.pallas_call(
        functools.partial(_outffn_kernel, nb=nb, tt=tt),
        grid=(bsz // nb, seq // tt),
        in_specs=[tok(D_MODEL), tok(A_WIDTH), tok(B_WIDTH),
                  pl.BlockSpec((nb, 6, D_MODEL), lambda b, t: (b, 0, 0)), once((1, D_MODEL)),
                  once((A_WIDTH, D_MODEL)), once((B_WIDTH, D_MODEL)), once((D_MODEL, D_FF)), once((D_FF, D_MODEL))],
        out_specs=tok(D_MODEL),
        out_shape=jax.ShapeDtypeStruct((bsz, seq, D_MODEL), F32),
        compiler_params=pltpu.CompilerParams(dimension_semantics=("arbitrary", "arbitrary"),
                                             vmem_limit_bytes=VMEM_LIMIT),
        name="outffn",
    )(x, oa, ob, mod, wts["g2"], wts["woa"], wts["wob"], wts["wup"], wts["wdn"])


def _prep_weights(norm1_g, norm2_g, w_in, conv_w, a_log, dt_bias, gdn_norm_g, qn_g, kn_g, w_out, w_up, w_down):
    pad_ba = jnp.zeros((D_MODEL, LANES - 2 * A_HEADS), F32)
    pad_row = lambda v: jnp.zeros((1, LANES), F32).at[0, A_HEADS:2 * A_HEADS].set(v)
    head = jnp.arange(B_WIDTH) // B_DH
    return dict(
        g1=norm1_g.reshape(1, D_MODEL), g2=norm2_g.reshape(1, D_MODEL),
        wqkv=w_in[:, QA:GA].astype(BF16), wgate=w_in[:, GA:BA].astype(BF16),
        wba=jnp.concatenate([w_in[:, BA:QB], pad_ba], axis=1).astype(BF16),
        wqb=w_in[:, QB:KB].astype(BF16), wkb=w_in[:, KB:VB].astype(BF16), wvb=w_in[:, VB:IN_COLS].astype(BF16),
        cw=conv_w, alog=pad_row(a_log), dtb=pad_row(dt_bias), ng=gdn_norm_g.reshape(1, A_DV),
        qng=jnp.tile(qn_g, B_HEADS).reshape(1, B_WIDTH), kng=jnp.tile(kn_g, B_HEADS).reshape(1, B_WIDTH),
        seg=(head[:, None] == head[None, :]).astype(BF16),
        woa=w_out[:A_WIDTH].astype(BF16), wob=w_out[A_WIDTH:].astype(BF16),
        wup=w_up.astype(BF16), wdn=w_down.astype(BF16))


def _layer(x, mod, left, s0, cache, bias, wts, *, nb, tt, chunk):
    bsz, seq, _ = x.shape
    qa, ka, va, sg, aux, qb, kb, vb, kb32, vb32, conv_state = _inproj(x, mod, left, wts, nb=nb, tt=tt)
    oa, s_new = _gdn(qa, ka, va, sg, aux, s0, wts["ng"], L=chunk)
    if cache is None:
        ob = _attn_prompt(qb, kb, vb, bias)
    else:
        ob = _attn_sample(qb, kb, vb, cache[0], cache[1], bias)
    y = _outffn(x, oa, ob, mod, wts, nb=nb, tt=tt)
    keep = kb32.shape[1]
    return (y, conv_state, s_new, kb32.reshape(bsz, keep, B_HEADS, B_DH), vb32.reshape(bsz, keep, B_HEADS, B_DH))


def kernel(x_prompt, x_sample, state_conv, state_gdn, cache_k_band, cache_v_band, c_prompt, c_sample,
           w_mod, b_mod, norm1_g, norm2_g, w_in, conv_w, a_log, dt_bias, gdn_norm_g, qn_g, kn_g,
           rel_bias, w_out, w_up, w_down):
    depth = w_mod.shape[0]
    bp, tp, _ = x_prompt.shape
    bs, ts, _ = x_sample.shape
    yp, ys = x_prompt, x_sample
    c_all = jnp.concatenate([c_prompt, c_sample], axis=0)
    outs = [[] for _ in range(8)]
    for l in range(depth):
        wts = _prep_weights(norm1_g[l], norm2_g[l], w_in[l], conv_w[l], a_log[l], dt_bias[l], gdn_norm_g[l],
                            qn_g[l], kn_g[l], w_out[l], w_up[l], w_down[l])
        mod = _modulation(c_all, w_mod[l], b_mod[l]).reshape(bp + bs, 6, D_MODEL)
        bias = _relbias(rel_bias[l])
        yp, cp, gp, kp, vp = _layer(
            yp, mod[:bp], jnp.zeros((bp, CONV_W - 1, CONV_CH), F32), jnp.zeros((bp, A_HEADS, A_DK, A_DV), F32),
            None, bias, wts, nb=1, tt=min(512, tp), chunk=min(CHUNK, tp))
        ys, cs, gs, ks, vs = _layer(
            ys, mod[bp:], state_conv[l], state_gdn[l], (cache_k_band[l], cache_v_band[l]), bias, wts,
            nb=bs, tt=ts, chunk=min(CHUNK, ts))
        for acc, val in zip(outs, (cp, gp, kp, vp, cs, gs, ks, vs)):
            acc.append(val)
    return (yp, ys) + tuple(jnp.stack(o) for o in outs)
```

```python
import functools

import jax
import jax.numpy as jnp
from jax import lax
from jax.experimental import pallas as pl
from jax.experimental.pallas import tpu as pltpu

D_MODEL = 1024
CHUNK = 64
A_HEADS = 4
A_DK = 128
A_DV = 128
A_WIDTH = A_HEADS * A_DV
CONV_W = 4
B_HEADS = 8
B_DH = 64
B_WIDTH = B_HEADS * B_DH
BAND_CHUNKS = 8
BAND_PAST = BAND_CHUNKS * CHUNK
BAND = BAND_PAST + CHUNK
REL_MAX = 4 * CHUNK
REL_SIZE = REL_MAX + CHUNK
D_FF = 4 * D_MODEL
EPS = 1e-6

QA = 0
KA = QA + A_HEADS * A_DK
VA = KA + A_HEADS * A_DK
GA = VA + A_WIDTH
BA = GA + A_WIDTH
AA = BA + A_HEADS
QB = AA + A_HEADS
KB = QB + B_WIDTH
VB = KB + B_WIDTH
IN_COLS = VB + B_WIDTH
CONV_CH = GA - QA

LANES = 128
SUBLANES = 8
INV_BLOCK = 16
GDN_G = 4
QBLK = 4 * CHUNK
WIN = BAND_PAST + QBLK
BIAS_EXT = 1024
CHUNK_SHIFT = CHUNK.bit_length() - 1
B_DH_SHIFT = B_DH.bit_length() - 1
LOG2E = 1.4426950408889634
MASKED = -1e30
VMEM_LIMIT = 56 * 1024 * 1024

F32 = jnp.float32
BF16 = jnp.bfloat16
HI = lax.Precision.HIGHEST


def _dot(a, b):
    return jnp.dot(a, b, preferred_element_type=F32)


def _dot_hi(a, b):
    return jnp.dot(a, b, preferred_element_type=F32, precision=HI)


def _dot_nt(a, b):
    return lax.dot_general(a, b, (((1,), (1,)), ((), ())), preferred_element_type=F32)


def _dot_tn(a, b):
    return lax.dot_general(a, b, (((0,), (0,)), ((), ())), preferred_element_type=F32)


def _sigmoid(x):
    return 1.0 / (1.0 + jnp.exp(-x))


def _silu(x):
    return x * _sigmoid(x)


def _softplus(x):
    return jnp.maximum(x, 0.0) + jnp.log(1.0 + jnp.exp(-jnp.abs(x)))


def _const_spec(shape):
    n = len(shape)
    return pl.BlockSpec(shape, lambda *_: (0,) * n)


def _mod_kernel(c_ref, w_ref, b_ref, o_ref):
    c = c_ref[...]
    o_ref[...] = _dot(_silu(c).astype(BF16), w_ref[...].astype(BF16)) + b_ref[...]


def _modulation(c, w_mod, b_mod):
    nb = c.shape[0]
    cols = w_mod.shape[1]
    bn = 512
    return pl.pallas_call(
        _mod_kernel,
        grid=(cols // bn,),
        in_specs=[pl.BlockSpec((nb, D_MODEL), lambda j: (0, 0)),
                  pl.BlockSpec((D_MODEL, bn), lambda j: (0, j)),
                  pl.BlockSpec((1, bn), lambda j: (0, j))],
        out_specs=pl.BlockSpec((nb, bn), lambda j: (0, j)),
        out_shape=jax.ShapeDtypeStruct((nb, cols), F32),
        compiler_params=pltpu.CompilerParams(dimension_semantics=("arbitrary",)),
        name="modulation",
    )(c, w_mod, b_mod.reshape(1, cols))


def _inproj_kernel(x_ref, mod_ref, left_ref, g1_ref, wqkv_ref, wgate_ref, wba_ref, wqb_ref, wkb_ref, wvb_ref,
                   cw_ref, alog_ref, dtb_ref, qng_ref, kng_ref, seg_ref,
                   qa_ref, ka_ref, va_ref, sg_ref, aux_ref, qb_ref, kb_ref, vb_ref, kb32_ref, vb32_ref, conv_ref,
                   buf_ref, *, nb, tt, nt, nkeep):
    t = pl.program_id(1)
    n = nb * tt

    @pl.when(t == 0)
    def _():
        buf_ref[:, 0:SUBLANES, :] = jnp.zeros((nb, SUBLANES, CONV_CH), F32)
        buf_ref[:, SUBLANES - (CONV_W - 1):SUBLANES, :] = left_ref[...]

    x = x_ref[...].reshape(n, D_MODEL)
    mod = mod_ref[...]
    sh1 = mod[:, 0:1, :]
    sc1 = mod[:, 1:2, :]
    ms = jnp.mean(x * x, axis=-1, keepdims=True)
    xn = x * lax.rsqrt(ms + EPS) * g1_ref[...]
    h = (xn.reshape(nb, tt, D_MODEL) * (1.0 + sc1) + sh1).reshape(n, D_MODEL)
    hb = h.astype(BF16)

    pre = _dot(hb, wqkv_ref[...]).reshape(nb, tt, CONV_CH)
    buf_ref[:, SUBLANES:SUBLANES + tt, :] = pre
    cw = cw_ref[...]
    conv = buf_ref[:, 5:5 + tt, :] * cw[0:1, :].reshape(1, 1, CONV_CH)
    conv = conv + buf_ref[:, 6:6 + tt, :] * cw[1:2, :].reshape(1, 1, CONV_CH)
    conv = conv + buf_ref[:, 7:7 + tt, :] * cw[2:3, :].reshape(1, 1, CONV_CH)
    conv = conv + pre * cw[3:4, :].reshape(1, 1, CONV_CH)
    conv_ref[...] = buf_ref[:, tt + 5:tt + SUBLANES, :]
    buf_ref[:, 0:SUBLANES, :] = buf_ref[:, tt:tt + SUBLANES, :]
    act = _silu(conv).reshape(n, CONV_CH)

    for hh in range(A_HEADS):
        lo = hh * A_DK
        qs = act[:, QA + lo:QA + lo + A_DK]
        ks = act[:, KA + lo:KA + lo + A_DK]
        qn = qs * lax.rsqrt(jnp.sum(qs * qs, axis=-1, keepdims=True) + EPS) * (A_DK ** -0.5)
        kn = ks * lax.rsqrt(jnp.sum(ks * ks, axis=-1, keepdims=True) + EPS)
        qa_ref[:, :, lo:lo + A_DK] = qn.astype(BF16).reshape(nb, tt, A_DK)
        ka_ref[:, :, lo:lo + A_DK] = kn.astype(BF16).reshape(nb, tt, A_DK)
    va_ref[...] = act[:, VA:GA].astype(BF16).reshape(nb, tt, A_WIDTH)

    gate = _dot(hb, wgate_ref[...])
    sg_ref[...] = _silu(gate).astype(BF16).reshape(nb, tt, A_WIDTH)

    ba = _dot(hb, wba_ref[...])
    lane = lax.broadcasted_iota(jnp.int32, (n, LANES), 1)
    beta = _sigmoid(ba)
    gdec = -jnp.exp(alog_ref[...]) * _softplus(ba + dtb_ref[...])
    aux = jnp.where(lane < A_HEADS, beta, jnp.where(lane < 2 * A_HEADS, gdec, 0.0))
    aux_ref[...] = aux.reshape(nb, tt, LANES)

    seg = seg_ref[...]
    pq = _dot(hb, wqb_ref[...])
    ssq = _dot((pq * pq).astype(BF16), seg)
    qb = pq * lax.rsqrt(ssq * (1.0 / B_DH) + EPS) * qng_ref[...] * (B_DH ** -0.5 * LOG2E)
    qb_ref[...] = qb.astype(BF16).reshape(nb, tt, B_WIDTH)
    pk = _dot(hb, wkb_ref[...])
    ssk = _dot((pk * pk).astype(BF16), seg)
    kb = pk * lax.rsqrt(ssk * (1.0 / B_DH) + EPS) * kng_ref[...]
    kb_ref[...] = kb.astype(BF16).reshape(nb, tt, B_WIDTH)
    vb = _dot(hb, wvb_ref[...])
    vb_ref[...] = vb.astype(BF16).reshape(nb, tt, B_WIDTH)

    @pl.when(t >= nt - nkeep)
    def _():
        kb32_ref[...] = kb.reshape(nb, tt, B_WIDTH)
        vb32_ref[...] = vb.reshape(nb, tt, B_WIDTH)


def _inproj(x, mod, left, wts, *, nb, tt):
    bsz, seq, _ = x.shape
    nt = seq // tt
    keep = min(BAND_PAST, seq)
    nkeep = keep // tt
    assert seq % tt == 0 and bsz % nb == 0 and keep % tt == 0 and tt % SUBLANES == 0
    grid = (bsz // nb, nt)
    tok = lambda w: pl.BlockSpec((nb, tt, w), lambda b, t: (b, t, 0))
    per_b = lambda r, w: pl.BlockSpec((nb, r, w), lambda b, t: (b, 0, 0))
    keep_spec = pl.BlockSpec((nb, tt, B_WIDTH), lambda b, t: (b, jnp.maximum(t - (nt - nkeep), 0), 0))
    bf = lambda w: jax.ShapeDtypeStruct((bsz, seq, w), BF16)
    out_shape = (bf(A_WIDTH), bf(A_WIDTH), bf(A_WIDTH), bf(A_WIDTH),
                 jax.ShapeDtypeStruct((bsz, seq, LANES), F32),
                 bf(B_WIDTH), bf(B_WIDTH), bf(B_WIDTH),
                 jax.ShapeDtypeStruct((bsz, keep, B_WIDTH), F32),
                 jax.ShapeDtypeStruct((bsz, keep, B_WIDTH), F32),
                 jax.ShapeDtypeStruct((bsz, CONV_W - 1, CONV_CH), F32))
    out_specs = (tok(A_WIDTH), tok(A_WIDTH), tok(A_WIDTH), tok(A_WIDTH), tok(LANES),
                 tok(B_WIDTH), tok(B_WIDTH), tok(B_WIDTH), keep_spec, keep_spec,
                 per_b(CONV_W - 1, CONV_CH))
    in_specs = [tok(D_MODEL), per_b(6, D_MODEL), per_b(CONV_W - 1, CONV_CH), _const_spec((1, D_MODEL)),
                _const_spec((D_MODEL, CONV_CH)), _const_spec((D_MODEL, A_WIDTH)), _const_spec((D_MODEL, LANES)),
                _const_spec((D_MODEL, B_WIDTH)), _const_spec((D_MODEL, B_WIDTH)), _const_spec((D_MODEL, B_WIDTH)),
                _const_spec((CONV_W, CONV_CH)), _const_spec((1, LANES)), _const_spec((1, LANES)),
                _const_spec((1, B_WIDTH)), _const_spec((1, B_WIDTH)), _const_spec((B_WIDTH, B_WIDTH))]
    return pl.pallas_call(
        functools.partial(_inproj_kernel, nb=nb, tt=tt, nt=nt, nkeep=nkeep),
        grid=grid, in_specs=in_specs, out_specs=out_specs, out_shape=out_shape,
        scratch_shapes=[pltpu.VMEM((nb, tt + SUBLANES, CONV_CH), F32)],
        compiler_params=pltpu.CompilerParams(dimension_semantics=("arbitrary", "arbitrary"),
                                             vmem_limit_bytes=VMEM_LIMIT),
        name="inproj",
    )(x, mod, left, wts["g1"], wts["wqkv"], wts["wgate"], wts["wba"], wts["wqb"], wts["wkb"], wts["wvb"],
      wts["cw"], wts["alog"], wts["dtb"], wts["qng"], wts["kng"], wts["seg"])


def _unit_lower_inverse(m, size):
    ri = lax.broadcasted_iota(jnp.int32, (size, size), 0)
    ci = lax.broadcasted_iota(jnp.int32, (size, size), 1)
    s = min(INV_BLOCK, size)
    sh = s.bit_length() - 1
    nd = jnp.where((ri >> sh) == (ci >> sh), -m, 0.0)
    x = jnp.where(ri == ci, 1.0, 0.0) + nd
    p = nd
    for _ in range(sh - 1):
        p = _dot_hi(p, p)
        x = x + _dot_hi(x, p)
    while s < size:
        same = (ri >> (sh + 1)) == (ci >> (sh + 1))
        low_left = (((ri >> sh) & 1) == 1) & (((ci >> sh) & 1) == 0)
        c = jnp.where(same & low_left, m, 0.0)
        x = x - _dot_hi(_dot_hi(x, c), x)
        s *= 2
        sh += 1
    return x


def _gdn_kernel(qa_ref, ka_ref, va_ref, sg_ref, aux_ref, s0_ref, ng_ref, oa_ref, sout_ref, s_ref, *, L, nc):
    c = pl.program_id(1)

    @pl.when(c == 0)
    def _():
        s_ref[...] = s0_ref[0]

    ri = lax.broadcasted_iota(jnp.int32, (L, L), 0)
    ci = lax.broadcasted_iota(jnp.int32, (L, L), 1)
    incl = ri >= ci
    strict = ri > ci
    aux = aux_ref[0]
    gc = _dot_hi(jnp.where(incl, 1.0, 0.0), aux)
    glast = gc[L - 1:L, :]
    eg = jnp.exp(gc)
    egl = jnp.exp(glast - gc)
    elast = jnp.exp(glast)
    gct = gc.T
    ng = ng_ref[...]

    for hh in range(A_HEADS):
        lo = hh * A_DK
        q = qa_ref[0, :, lo:lo + A_DK].astype(F32)
        k = ka_ref[0, :, lo:lo + A_DK].astype(F32)
        v = va_ref[0, :, lo:lo + A_DV].astype(F32)
        beta_c = aux[:, hh:hh + 1]
        gc_c = gc[:, A_HEADS + hh:A_HEADS + hh + 1]
        eg_c = eg[:, A_HEADS + hh:A_HEADS + hh + 1]
        egl_c = egl[:, A_HEADS + hh:A_HEADS + hh + 1]
        el = elast[:, A_HEADS + hh:A_HEADS + hh + 1]
        gc_r = gct[A_HEADS + hh:A_HEADS + hh + 1, :]
        decay = jnp.exp(jnp.where(incl, gc_c - gc_r, -jnp.inf))
        kb16 = k.astype(BF16)
        qkk = _dot_nt(jnp.concatenate([q, k], axis=0).astype(BF16), kb16)
        qk = qkk[0:L] * decay
        m = jnp.where(strict, qkk[L:2 * L] * beta_c * decay, 0.0)
        tinv = _unit_lower_inverse(m, L)
        rhs = jnp.concatenate([v * beta_c, k * (beta_c * eg_c)], axis=1)
        uw = _dot_hi(tinv, rhs)
        u = uw[:, 0:A_DV]
        w = uw[:, A_DV:A_DV + A_DK]
        s_h = s_ref[hh]
        res = _dot(jnp.concatenate([w, q * eg_c], axis=0).astype(BF16), s_h.astype(BF16))
        vnew = u - res[0:L]
        vnew16 = vnew.astype(BF16)
        o = res[L:2 * L] + _dot(qk.astype(BF16), vnew16)
        s_ref[hh] = s_h * el + _dot_tn((k * egl_c).astype(BF16), vnew16)
        on = o * lax.rsqrt(jnp.mean(o * o, axis=-1, keepdims=True) + EPS) * ng
        sg = sg_ref[0, :, lo:lo + A_DV].astype(F32)
        oa_ref[0, :, lo:lo + A_DV] = (on * sg).astype(BF16)

    @pl.when(c == nc - 1)
    def _():
        sout_ref[0] = s_ref[...]


def _gdn(qa, ka, va, sg, aux, s0, ng, *, L):
    bsz, seq, _ = qa.shape
    nc = seq // L
    assert seq % L == 0
    tok = lambda w: pl.BlockSpec((1, L, w), lambda b, c: (b, c, 0))
    st = pl.BlockSpec((1, A_HEADS, A_DK, A_DV), lambda b, c: (b, 0, 0, 0))
    return pl.pallas_call(
        functools.partial(_gdn_kernel, L=L, nc=nc),
        grid=(bsz, nc),
        in_specs=[tok(A_WIDTH), tok(A_WIDTH), tok(A_WIDTH), tok(A_WIDTH), tok(LANES), st, _const_spec((1, A_DV))],
        out_specs=(tok(A_WIDTH), st),
        out_shape=(jax.ShapeDtypeStruct((bsz, seq, A_WIDTH), BF16),
                   jax.ShapeDtypeStruct((bsz, A_HEADS, A_DK, A_DV), F32)),
        scratch_shapes=[pltpu.VMEM((A_HEADS, A_DK, A_DV), F32)],
        compiler_params=pltpu.CompilerParams(dimension_semantics=("arbitrary", "arbitrary"),
                                             vmem_limit_bytes=VMEM_LIMIT),
        name="gdn",
    )(qa, ka, va, sg, aux, s0, ng)


def _pair_blockdiag(p, lo):
    z = jnp.zeros_like(p)
    return jnp.concatenate([jnp.where(lo, p, z), jnp.where(lo, z, p)], axis=0)


def _pair_matmul(a, b, lo):
    return _dot(a.astype(BF16), _pair_blockdiag(b, lo).astype(BF16))


def _pair_unit_lower_inverse(ms, ri, cl, lo, tick):
    sh = INV_BLOCK.bit_length() - 1
    nds = [jnp.where((ri >> sh) == (cl >> sh), -m, 0.0) for m in ms]
    xs = [jnp.where(ri == cl, 1.0, 0.0) + nd for nd in nds]
    ps = [_pair_matmul(nd, nd, lo) for nd in nds]
    tick()
    for lvl in range(sh - 1):
        xs = [x + _pair_matmul(x, p, lo) for x, p in zip(xs, ps)]
        if lvl < sh - 2:
            ps = [_pair_matmul(p, p, lo) for p in ps]
        tick()
    s = INV_BLOCK
    while s < CHUNK:
        same = (ri >> (sh + 1)) == (cl >> (sh + 1))
        low_left = (((ri >> sh) & 1) == 1) & (((cl >> sh) & 1) == 0)
        ys = [_pair_matmul(x, jnp.where(same & low_left, m, 0.0), lo) for x, m in zip(xs, ms)]
        tick()
        xs = [x - _pair_matmul(y, x, lo) for x, y in zip(xs, ys)]
        tick()
        s *= 2
        sh += 1
    return xs


def _gdn_prepare(chunks, between):
    L = CHUNK
    pending = list(between)

    def tick():
        if pending:
            pending.pop(0)()

    ri = lax.broadcasted_iota(jnp.int32, (L, LANES), 0)
    lane = lax.broadcasted_iota(jnp.int32, (L, LANES), 1)
    cl = lane & (L - 1)
    lo = lane < L
    lo2 = lax.broadcasted_iota(jnp.int32, (2 * L, LANES), 1) < L
    tri = jnp.where(lax.broadcasted_iota(jnp.int32, (L, L), 0) >= lax.broadcasted_iota(jnp.int32, (L, L), 1), 1.0, 0.0)
    col = lambda x, j: x[:, j:j + 1]
    chains = [(g, pp) for g in range(len(chunks)) for pp in range(A_HEADS // 2)]
    hs_of = lambda pp: (2 * pp, 2 * pp + 1)
    auxs = [c[3] for c in chunks]

    gcs = [_dot_hi(tri, a) for a in auxs]
    g_pairs = [jnp.where(lo, col(auxs[g], A_HEADS + hs_of(pp)[0]), col(auxs[g], A_HEADS + hs_of(pp)[1]))
               for g, pp in chains]
    gdiffs = [_dot_hi(tri, jnp.where(ri > cl, gp, 0.0)) for gp in g_pairs]
    tick()
    qhs = [[chunks[g][0][:, h * A_DK:(h + 1) * A_DK] for h in hs_of(pp)] for g, pp in chains]
    khs = [[chunks[g][1][:, h * A_DK:(h + 1) * A_DK] for h in hs_of(pp)] for g, pp in chains]
    vhs = [[chunks[g][2][:, h * A_DV:(h + 1) * A_DV] for h in hs_of(pp)] for g, pp in chains]
    ksts = [jnp.concatenate(kh, axis=0) for kh in khs]
    r0s = [_dot_nt(jnp.concatenate([qh[0], kh[0]], axis=0), kst) for qh, kh, kst in zip(qhs, khs, ksts)]
    r1s = [_dot_nt(jnp.concatenate([qh[1], kh[1]], axis=0), kst) for qh, kh, kst in zip(qhs, khs, ksts)]
    tick()
    glasts = [gc[L - 1:L, :] for gc in gcs]
    egs = [jnp.exp(gc) for gc in gcs]
    egls = [jnp.exp(gl - gc) for gl, gc in zip(glasts, gcs)]
    decays = [jnp.where(ri >= cl, jnp.exp(gd), 0.0) for gd in gdiffs]
    grams = [jnp.where(lo2, r0, r1) for r0, r1 in zip(r0s, r1s)]
    qks = [gram[0:L] * dec for gram, dec in zip(grams, decays)]
    beta_pairs = [jnp.where(lo, col(auxs[g], hs_of(pp)[0]), col(auxs[g], hs_of(pp)[1])) for g, pp in chains]
    ms = [jnp.where(ri > cl, gram[L:2 * L] * bp * dec, 0.0) for gram, bp, dec in zip(grams, beta_pairs, decays)]
    tinvs = _pair_unit_lower_inverse(ms, ri, cl, lo, tick)

    uws, kdts = [], []
    for (g, pp), kh, vh, tinv in zip(chains, khs, vhs, tinvs):
        aux, eg, egl = auxs[g], egs[g], egls[g]
        kf = [x.astype(F32) for x in kh]
        rhs = jnp.concatenate(
            [jnp.concatenate([vh[e].astype(F32) * col(aux, h), kf[e] * (col(aux, h) * col(eg, A_HEADS + h))], axis=1)
             for e, h in enumerate(hs_of(pp))], axis=0)
        z = jnp.zeros_like(tinv)
        tsel = jnp.concatenate([jnp.where(lo, tinv, z), jnp.where(lo, z, tinv)], axis=0)
        uws.append(_dot(tsel.astype(BF16), rhs.astype(BF16)).astype(BF16))
        kdts.append(jnp.concatenate([kf[e] * col(egl, A_HEADS + h) for e, h in enumerate(hs_of(pp))], axis=0).T)
    tick()
    r6s = []
    for qk, kdt, uw in zip(qks, kdts, uws):
        zq = jnp.zeros_like(qk)
        zk = jnp.zeros_like(kdt)
        lhs = jnp.concatenate([jnp.where(lo, qk, zq), jnp.where(lo2, kdt, zk),
                               jnp.where(lo, zq, qk), jnp.where(lo2, zk, kdt)], axis=0)
        r6s.append(_dot(lhs.astype(BF16), uw))
    while pending:
        tick()
    out = [([None] * A_HEADS, jnp.exp(gl)) for gl in glasts]
    for (g, pp), qh, r6 in zip(chains, qhs, r6s):
        for e, h in enumerate(hs_of(pp)):
            base = e * (L + A_DK)
            o0 = r6[base:base + L, 0:A_DV]
            qkw = r6[base:base + L, A_DV:A_DV + A_DK]
            bm = r6[base + L:base + L + A_DK, 0:A_DV]
            cm = r6[base + L:base + L + A_DK, A_DV:A_DV + A_DK]
            qt = qh[e].astype(F32) * col(egs[g], A_HEADS + h) - qkw
            out[g][0][h] = (jnp.concatenate([cm, qt], axis=0).astype(BF16), bm, o0)
    return out


def _gdn_pipe_kernel(qa_ref, ka_ref, va_ref, aux_ref, sg_ref, s0_ref, ng_ref, oa_ref, sout_ref,
                     s_ref, cq_ref, b_ref, o0_ref, el_ref, *, nsteps):
    i = pl.program_id(0)
    L = CHUNK
    wr = i % 2
    rd = (i + 1) % 2

    @pl.when(i == 0)
    def _():
        s_ref[...] = jnp.zeros(s_ref.shape, F32)
        cq_ref[1] = jnp.zeros(cq_ref.shape[1:], BF16)
        b_ref[1] = jnp.zeros(b_ref.shape[1:], F32)
        o0_ref[1] = jnp.zeros(o0_ref.shape[1:], F32)
        el_ref[1] = jnp.zeros(el_ref.shape[1:], F32)

    j = jnp.maximum(i - 1, 0)
    first = (j % nsteps) == 0
    ng = ng_ref[...]
    state = [jnp.where(first, s0_ref[0, h], s_ref[h]) for h in range(A_HEADS)]

    def recur(g):
        def run():
            el = el_ref[rd, g, 0:1, :]
            rs = [_dot(cq_ref[rd, g, h], state[h].astype(BF16)) for h in range(A_HEADS)]
            for h in range(A_HEADS):
                o = rs[h][A_DK:A_DK + L] + o0_ref[rd, g, h]
                state[h] = state[h] * el[:, A_HEADS + h:A_HEADS + h + 1] - rs[h][0:A_DK] + b_ref[rd, g, h]
                on = o * lax.rsqrt(jnp.mean(o * o, axis=-1, keepdims=True) + EPS) * ng
                sg = sg_ref[0, g * L:(g + 1) * L, h * A_DV:(h + 1) * A_DV].astype(F32)
                oa_ref[0, g * L:(g + 1) * L, h * A_DV:(h + 1) * A_DV] = (on * sg).astype(BF16)
        return run

    chunks = [(qa_ref[0, g * L:(g + 1) * L, :], ka_ref[0, g * L:(g + 1) * L, :], va_ref[0, g * L:(g + 1) * L, :],
               aux_ref[0, g * L:(g + 1) * L, :]) for g in range(GDN_G)]
    prepared = _gdn_prepare(chunks, [recur(g) for g in range(GDN_G)])

    for h in range(A_HEADS):
        s_ref[h] = state[h]

    @pl.when((i >= 1) & ((j % nsteps) == nsteps - 1))
    def _():
        for h in range(A_HEADS):
            sout_ref[0, h] = state[h]

    for g, (heads, el) in enumerate(prepared):
        el_ref[wr, g] = jnp.broadcast_to(el, (SUBLANES, LANES))
        for h in range(A_HEADS):
            cq_ref[wr, g, h] = heads[h][0]
            b_ref[wr, g, h] = heads[h][1]
            o0_ref[wr, g, h] = heads[h][2]


def _gdn_pipe(qa, ka, va, sg, aux, s0, ng):
    bsz, seq, _ = qa.shape
    rows = GDN_G * CHUNK
    assert seq % rows == 0
    nsteps = seq // rows
    total = bsz * nsteps
    cur = lambda w: pl.BlockSpec((1, rows, w), lambda i: (jnp.minimum(i, total - 1) // nsteps,
                                                          jnp.minimum(i, total - 1) % nsteps, 0))
    prev = lambda w: pl.BlockSpec((1, rows, w), lambda i: (jnp.maximum(i - 1, 0) // nsteps,
                                                           jnp.maximum(i - 1, 0) % nsteps, 0))
    st = pl.BlockSpec((1, A_HEADS, A_DK, A_DV), lambda i: (jnp.maximum(i - 1, 0) // nsteps, 0, 0, 0))
    return pl.pallas_call(
        functools.partial(_gdn_pipe_kernel, nsteps=nsteps),
        grid=(total + 1,),
        in_specs=[cur(A_WIDTH), cur(A_WIDTH), cur(A_WIDTH), cur(LANES), prev(A_WIDTH), st, _const_spec((1, A_DV))],
        out_specs=(prev(A_WIDTH), st),
        out_shape=(jax.ShapeDtypeStruct((bsz, seq, A_WIDTH), BF16),
                   jax.ShapeDtypeStruct((bsz, A_HEADS, A_DK, A_DV), F32)),
        scratch_shapes=[pltpu.VMEM((A_HEADS, A_DK, A_DV), F32),
                        pltpu.VMEM((2, GDN_G, A_HEADS, A_DK + CHUNK, A_DV), BF16),
                        pltpu.VMEM((2, GDN_G, A_HEADS, A_DK, A_DV), F32),
                        pltpu.VMEM((2, GDN_G, A_HEADS, CHUNK, A_DV), F32),
                        pltpu.VMEM((2, GDN_G, SUBLANES, LANES), F32)],
        compiler_params=pltpu.CompilerParams(dimension_semantics=("arbitrary",), vmem_limit_bytes=VMEM_LIMIT),
        name="gdn_pipe",
    )(qa, ka, va, aux, sg, s0, ng)


def _relbias_kernel(rb_ref, o_ref):
    r = lax.broadcasted_iota(jnp.int32, (REL_SIZE, BIAS_EXT), 0)
    p = lax.broadcasted_iota(jnp.int32, (REL_SIZE, BIAS_EXT), 1)
    moff = jnp.where(p < WIN, p, p - BIAS_EXT)
    idx = jnp.clip(BAND_PAST - moff, -(CHUNK - 1), REL_MAX) + (CHUNK - 1)
    base = _dot_hi(rb_ref[...], jnp.where(r == idx, 1.0, 0.0)) * LOG2E
    qi = lax.broadcasted_iota(jnp.int32, (QBLK, WIN), 0) >> CHUNK_SHIFT
    kj = lax.broadcasted_iota(jnp.int32, (QBLK, WIN), 1) >> CHUNK_SHIFT
    in_band = (kj >= qi) & (kj <= qi + BAND_CHUNKS)
    for hh in range(B_HEADS):
        rows = jnp.broadcast_to(base[hh:hh + 1, :], (QBLK, BIAS_EXT))
        toep = pltpu.roll(rows, 0, 1, stride=1, stride_axis=0)
        o_ref[hh] = jnp.where(in_band, toep[:, 0:WIN], MASKED)


def _relbias(rel_bias):
    return pl.pallas_call(
        _relbias_kernel,
        in_specs=[pl.BlockSpec((B_HEADS, REL_SIZE), lambda: (0, 0))],
        out_specs=pl.BlockSpec((B_HEADS, QBLK, WIN), lambda: (0, 0, 0)),
        out_shape=jax.ShapeDtypeStruct((B_HEADS, QBLK, WIN), F32),
        compiler_params=pltpu.CompilerParams(vmem_limit_bytes=VMEM_LIMIT),
        name="relbias",
    )(rel_bias)


def _attn_prompt_kernel(q_ref, k_ref, v_ref, bias_ref, o_ref, kaug_ref, vaug_ref, *, seq):
    c = pl.program_id(1)

    @pl.when(c == 0)
    def _():
        lane = lax.broadcasted_iota(jnp.int32, (seq, B_WIDTH), 1)
        even = ((lane >> B_DH_SHIFT) & 1) == 0
        k = k_ref[0]
        v = v_ref[0]
        for e in range(2):
            mine = even if e == 0 else jnp.logical_not(even)
            kaug_ref[e, 0:BAND_PAST, :] = jnp.zeros((BAND_PAST, B_WIDTH), BF16)
            vaug_ref[e, 0:BAND_PAST, :] = jnp.zeros((BAND_PAST, B_WIDTH), BF16)
            kaug_ref[e, BAND_PAST:BAND_PAST + seq, :] = jnp.where(mine, k, jnp.zeros_like(k))
            vaug_ref[e, BAND_PAST:BAND_PAST + seq, :] = jnp.where(mine, v, jnp.ones_like(v))

    start = pl.multiple_of(c * QBLK, QBLK)
    lane = lax.broadcasted_iota(jnp.int32, (QBLK, LANES), 1)
    for pp in range(B_HEADS // 2):
        lo = pp * LANES
        qp = q_ref[0, :, lo:lo + LANES]
        res = []
        for e in range(2):
            kk = kaug_ref[e, pl.ds(start, WIN), lo:lo + LANES]
            vv = vaug_ref[e, pl.ds(start, WIN), lo:lo + LANES]
            s = _dot_nt(qp, kk) + bias_ref[2 * pp + e]
            pr = jnp.exp2(s - jnp.max(s, axis=-1, keepdims=True)).astype(BF16)
            res.append(_dot(pr, vv))
        num = jnp.where(lane < B_DH, res[0], res[1])
        den = pltpu.roll(jnp.where(lane < B_DH, res[1], res[0]), B_DH, 1)
        o_ref[0, :, lo:lo + LANES] = (num / den).astype(BF16)


def _attn_prompt(qb, kb, vb, bias):
    bsz, seq, _ = qb.shape
    assert seq % QBLK == 0
    tok = pl.BlockSpec((1, QBLK, B_WIDTH), lambda b, c: (b, c, 0))
    full = pl.BlockSpec((1, seq, B_WIDTH), lambda b, c: (b, 0, 0))
    bias_spec = pl.BlockSpec((B_HEADS, QBLK, WIN), lambda b, c: (0, 0, 0), pipeline_mode=pl.Buffered(1))
    return pl.pallas_call(
        functools.partial(_attn_prompt_kernel, seq=seq),
        grid=(bsz, seq // QBLK),
        in_specs=[tok, full, full, bias_spec],
        out_specs=tok,
        out_shape=jax.ShapeDtypeStruct((bsz, seq, B_WIDTH), BF16),
        scratch_shapes=[pltpu.VMEM((2, BAND_PAST + seq, B_WIDTH), BF16),
                        pltpu.VMEM((2, BAND_PAST + seq, B_WIDTH), BF16)],
        compiler_params=pltpu.CompilerParams(dimension_semantics=("arbitrary", "arbitrary"),
                                             vmem_limit_bytes=VMEM_LIMIT),
        name="attn_prompt",
    )(qb, kb, vb, bias)


def _attn_sample_kernel(q_ref, kn_ref, vn_ref, kc_ref, vc_ref, bias_ref, o_ref, *, lc, tq):
    kc = kc_ref[0].astype(BF16)
    vc = vc_ref[0].astype(BF16)
    off = BAND_PAST - lc
    lane = lax.broadcasted_iota(jnp.int32, (tq, LANES), 1)
    for pp in range(B_HEADS // 2):
        lo = pp * LANES
        qp = q_ref[0, :, lo:lo + LANES]
        keys = [kc[:, lo:lo + LANES], kn_ref[0, :, lo:lo + LANES]]
        vals = [vc[:, lo:lo + LANES], vn_ref[0, :, lo:lo + LANES]]
        out = jnp.zeros((tq, LANES), F32)
        for e in range(2):
            hh = 2 * pp + e
            mine = (lane >= e * B_DH) & (lane < (e + 1) * B_DH)
            qm = jnp.where(mine, qp, jnp.zeros_like(qp))
            ss = [_dot_nt(qm, keys[0]) + bias_ref[hh, 0:tq, off:off + lc],
                  _dot_nt(qm, keys[1]) + bias_ref[hh, 0:tq, BAND_PAST:BAND_PAST + tq]]
            mx = jnp.maximum(jnp.max(ss[0], axis=-1, keepdims=True), jnp.max(ss[1], axis=-1, keepdims=True))
            es = [jnp.exp2(s - mx) for s in ss]
            den = jnp.sum(es[0], axis=-1, keepdims=True) + jnp.sum(es[1], axis=-1, keepdims=True)
            acc = _dot((es[0] / den).astype(BF16), vals[0]) + _dot((es[1] / den).astype(BF16), vals[1])
            out = out + jnp.where(mine, acc, 0.0)
        o_ref[0, :, lo:lo + LANES] = out.astype(BF16)


def _attn_sample(qb, kb, vb, k_cache, v_cache, bias):
    bsz, tq, _ = qb.shape
    lc = k_cache.shape[1]
    assert lc <= BAND_PAST and tq <= CHUNK
    tok = pl.BlockSpec((1, tq, B_WIDTH), lambda b: (b, 0, 0))
    cache = pl.BlockSpec((1, lc, B_WIDTH), lambda b: (b, 0, 0))
    bias_spec = pl.BlockSpec((B_HEADS, QBLK, WIN), lambda b: (0, 0, 0), pipeline_mode=pl.Buffered(1))
    return pl.pallas_call(
        functools.partial(_attn_sample_kernel, lc=lc, tq=tq),
        grid=(bsz,),
        in_specs=[tok, tok, tok, cache, cache, bias_spec],
        out_specs=tok,
        out_shape=jax.ShapeDtypeStruct((bsz, tq, B_WIDTH), BF16),
        compiler_params=pltpu.CompilerParams(dimension_semantics=("arbitrary",), vmem_limit_bytes=VMEM_LIMIT),
        name="attn_sample",
    )(qb, kb, vb, k_cache.reshape(bsz, lc, B_WIDTH), v_cache.reshape(bsz, lc, B_WIDTH), bias)


def _outffn_kernel(x_ref, oa_ref, ob_ref, mod_ref, g2_ref, woa_ref, wob_ref, wup_ref, wdn_ref, y_ref, *, nb, tt):
    n = nb * tt
    mod = mod_ref[...]
    gate1 = mod[:, 2:3, :]
    sh2 = mod[:, 3:4, :]
    sc2 = mod[:, 4:5, :]
    gate2 = mod[:, 5:6, :]
    mix = _dot(oa_ref[...].reshape(n, A_WIDTH), woa_ref[...]) + _dot(ob_ref[...].reshape(n, B_WIDTH), wob_ref[...])
    y1 = x_ref[...] + gate1 * mix.reshape(nb, tt, D_MODEL)
    y1f = y1.reshape(n, D_MODEL)
    ms = jnp.mean(y1f * y1f, axis=-1, keepdims=True)
    yn = y1f * lax.rsqrt(ms + EPS) * g2_ref[...]
    h2 = (yn.reshape(nb, tt, D_MODEL) * (1.0 + sc2) + sh2).reshape(n, D_MODEL).astype(BF16)
    acc = jnp.zeros((n, D_MODEL), F32)
    fc = D_MODEL
    for jj in range(D_FF // fc):
        u = _dot(h2, wup_ref[:, jj * fc:(jj + 1) * fc])
        r = jnp.maximum(u, 0.0)
        acc = acc + _dot((r * r).astype(BF16), wdn_ref[jj * fc:(jj + 1) * fc, :])
    y_ref[...] = y1 + gate2 * acc.reshape(nb, tt, D_MODEL)


def _outffn(x, oa, ob, mod, wts, *, nb, tt):
    bsz, seq, _ = x.shape
    assert seq % tt == 0 and bsz % nb == 0
    tok = lambda w: pl.BlockSpec((nb, tt, w), lambda b, t: (b, t, 0))
    once = lambda shape: pl.BlockSpec(shape, lambda b, t: (0,) * len(shape), pipeline_mode=pl.Buffered(1))
    return pl.pallas_call(
        functools.partial(_outffn_kernel, nb=nb, tt=tt),
        grid=(bsz // nb, seq // tt),
        in_specs=[tok(D_MODEL), tok(A_WIDTH), tok(B_WIDTH),
                  pl.BlockSpec((nb, 6, D_MODEL), lambda b, t: (b, 0, 0)), once((1, D_MODEL)),
                  once((A_WIDTH, D_MODEL)), once((B_WIDTH, D_MODEL)), once((D_MODEL, D_FF)), once((D_FF, D_MODEL))],
        out_specs=tok(D_MODEL),
        out_shape=jax.ShapeDtypeStruct((bsz, seq, D_MODEL), F32),
        compiler_params=pltpu.CompilerParams(dimension_semantics=("arbitrary", "arbitrary"),
                                             vmem_limit_bytes=VMEM_LIMIT),
        name="outffn",
    )(x, oa, ob, mod, wts["g2"], wts["woa"], wts["wob"], wts["wup"], wts["wdn"])


def _prep_weights(norm1_g, norm2_g, w_in, conv_w, a_log, dt_bias, gdn_norm_g, qn_g, kn_g, w_out, w_up, w_down):
    pad_ba = jnp.zeros((D_MODEL, LANES - 2 * A_HEADS), F32)
    pad_row = lambda v: jnp.zeros((1, LANES), F32).at[0, A_HEADS:2 * A_HEADS].set(v)
    head = jnp.arange(B_WIDTH) // B_DH
    return dict(
        g1=norm1_g.reshape(1, D_MODEL), g2=norm2_g.reshape(1, D_MODEL),
        wqkv=w_in[:, QA:GA].astype(BF16), wgate=w_in[:, GA:BA].astype(BF16),
        wba=jnp.concatenate([w_in[:, BA:QB], pad_ba], axis=1).astype(BF16),
        wqb=w_in[:, QB:KB].astype(BF16), wkb=w_in[:, KB:VB].astype(BF16), wvb=w_in[:, VB:IN_COLS].astype(BF16),
        cw=conv_w, alog=pad_row(a_log), dtb=pad_row(dt_bias), ng=gdn_norm_g.reshape(1, A_DV),
        qng=jnp.tile(qn_g, B_HEADS).reshape(1, B_WIDTH), kng=jnp.tile(kn_g, B_HEADS).reshape(1, B_WIDTH),
        seg=(head[:, None] == head[None, :]).astype(BF16),
        woa=w_out[:A_WIDTH].astype(BF16), wob=w_out[A_WIDTH:].astype(BF16),
        wup=w_up.astype(BF16), wdn=w_down.astype(BF16))


def _layer(x, mod, left, s0, cache, bias, wts, *, nb, tt, chunk):
    bsz, seq, _ = x.shape
    qa, ka, va, sg, aux, qb, kb, vb, kb32, vb32, conv_state = _inproj(x, mod, left, wts, nb=nb, tt=tt)
    if chunk == CHUNK and seq % (GDN_G * CHUNK) == 0:
        oa, s_new = _gdn_pipe(qa, ka, va, sg, aux, s0, wts["ng"])
    else:
        oa, s_new = _gdn(qa, ka, va, sg, aux, s0, wts["ng"], L=chunk)
    if cache is None:
        ob = _attn_prompt(qb, kb, vb, bias)
    else:
        ob = _attn_sample(qb, kb, vb, cache[0], cache[1], bias)
    y = _outffn(x, oa, ob, mod, wts, nb=nb, tt=tt)
    keep = kb32.shape[1]
    return (y, conv_state, s_new, kb32.reshape(bsz, keep, B_HEADS, B_DH), vb32.reshape(bsz, keep, B_HEADS, B_DH))


def kernel(x_prompt, x_sample, state_conv, state_gdn, cache_k_band, cache_v_band, c_prompt, c_sample,
           w_mod, b_mod, norm1_g, norm2_g, w_in, conv_w, a_log, dt_bias, gdn_norm_g, qn_g, kn_g,
           rel_bias, w_out, w_up, w_down):
    depth = w_mod.shape[0]
    bp, tp, _ = x_prompt.shape
    bs, ts, _ = x_sample.shape
    yp, ys = x_prompt, x_sample
    c_all = jnp.concatenate([c_prompt, c_sample], axis=0)
    outs = [[] for _ in range(8)]
    for l in range(depth):
        wts = _prep_weights(norm1_g[l], norm2_g[l], w_in[l], conv_w[l], a_log[l], dt_bias[l], gdn_norm_g[l],
                            qn_g[l], kn_g[l], w_out[l], w_up[l], w_down[l])
        mod = _modulation(c_all, w_mod[l], b_mod[l]).reshape(bp + bs, 6, D_MODEL)
        bias = _relbias(rel_bias[l])
        yp, cp, gp, kp, vp = _layer(
            yp, mod[:bp], jnp.zeros((bp, CONV_W - 1, CONV_CH), F32), jnp.zeros((bp, A_HEADS, A_DK, A_DV), F32),
            None, bias, wts, nb=1, tt=min(512, tp), chunk=min(CHUNK, tp))
        ys, cs, gs, ks, vs = _layer(
            ys, mod[bp:], state_conv[l], state_gdn[l], (cache_k_band[l], cache_v_band[l]), bias, wts,
            nb=bs, tt=ts, chunk=min(CHUNK, ts))
        for acc, val in zip(outs, (cp, gp, kp, vp, cs, gs, ks, vs)):
            acc.append(val)
    return (yp, ys) + tuple(jnp.stack(o) for o in outs)
```

```python
import functools

import jax
import jax.numpy as jnp
from jax import lax
from jax.experimental import pallas as pl
from jax.experimental.pallas import tpu as pltpu

D_MODEL = 1024
CHUNK = 64
A_HEADS = 4
A_DK = 128
A_DV = 128
A_WIDTH = A_HEADS * A_DV
CONV_W = 4
B_HEADS = 8
B_DH = 64
B_WIDTH = B_HEADS * B_DH
BAND_CHUNKS = 8
BAND_PAST = BAND_CHUNKS * CHUNK
BAND = BAND_PAST + CHUNK
REL_MAX = 4 * CHUNK
REL_SIZE = REL_MAX + CHUNK
D_FF = 4 * D_MODEL
EPS = 1e-6

QA = 0
KA = QA + A_HEADS * A_DK
VA = KA + A_HEADS * A_DK
GA = VA + A_WIDTH
BA = GA + A_WIDTH
AA = BA + A_HEADS
QB = AA + A_HEADS
KB = QB + B_WIDTH
VB = KB + B_WIDTH
IN_COLS = VB + B_WIDTH
CONV_CH = GA - QA

LANES = 128
SUBLANES = 8
INV_BLOCK = 16
GDN_G = 4
QBLK = 4 * CHUNK
WIN = BAND_PAST + QBLK
BIAS_EXT = 1024
CHUNK_SHIFT = CHUNK.bit_length() - 1
B_DH_SHIFT = B_DH.bit_length() - 1
LOG2E = 1.4426950408889634
MASKED = -1e30
VMEM_LIMIT = 56 * 1024 * 1024

F32 = jnp.float32
BF16 = jnp.bfloat16
HI = lax.Precision.HIGHEST


def _dot(a, b):
    return jnp.dot(a, b, preferred_element_type=F32)


def _dot_hi(a, b):
    return jnp.dot(a, b, preferred_element_type=F32, precision=HI)


def _dot_nt(a, b):
    return lax.dot_general(a, b, (((1,), (1,)), ((), ())), preferred_element_type=F32)


def _dot_tn(a, b):
    return lax.dot_general(a, b, (((0,), (0,)), ((), ())), preferred_element_type=F32)


def _sigmoid(x):
    return 1.0 / (1.0 + jnp.exp(-x))


def _silu(x):
    return x * _sigmoid(x)


def _softplus(x):
    return jnp.maximum(x, 0.0) + jnp.log(1.0 + jnp.exp(-jnp.abs(x)))


def _const_spec(shape):
    n = len(shape)
    return pl.BlockSpec(shape, lambda *_: (0,) * n)


def _mod_kernel(c_ref, w_ref, b_ref, o_ref):
    c = c_ref[...]
    o_ref[...] = _dot(_silu(c).astype(BF16), w_ref[...].astype(BF16)) + b_ref[...]


def _modulation(c, w_mod, b_mod):
    nb = c.shape[0]
    cols = w_mod.shape[1]
    bn = 512
    return pl.pallas_call(
        _mod_kernel,
        grid=(cols // bn,),
        in_specs=[pl.BlockSpec((nb, D_MODEL), lambda j: (0, 0)),
                  pl.BlockSpec((D_MODEL, bn), lambda j: (0, j)),
                  pl.BlockSpec((1, bn), lambda j: (0, j))],
        out_specs=pl.BlockSpec((nb, bn), lambda j: (0, j)),
        out_shape=jax.ShapeDtypeStruct((nb, cols), F32),
        compiler_params=pltpu.CompilerParams(dimension_semantics=("arbitrary",)),
        name="modulation",
    )(c, w_mod, b_mod.reshape(1, cols))


def _inproj_kernel(x_ref, mod_ref, left_ref, g1_ref, wqkv_ref, wgate_ref, wba_ref, wqb_ref, wkb_ref, wvb_ref,
                   cw_ref, alog_ref, dtb_ref, qng_ref, kng_ref, seg_ref,
                   qa_ref, ka_ref, va_ref, sg_ref, aux_ref, qb_ref, kb_ref, vb_ref, kb32_ref, vb32_ref, conv_ref,
                   buf_ref, *, nb, tt, nt, nkeep):
    t = pl.program_id(1)
    n = nb * tt

    @pl.when(t == 0)
    def _():
        buf_ref[:, 0:SUBLANES, :] = jnp.zeros((nb, SUBLANES, CONV_CH), F32)
        buf_ref[:, SUBLANES - (CONV_W - 1):SUBLANES, :] = left_ref[...]

    x = x_ref[...].reshape(n, D_MODEL)
    mod = mod_ref[...]
    sh1 = mod[:, 0:1, :]
    sc1 = mod[:, 1:2, :]
    ms = jnp.mean(x * x, axis=-1, keepdims=True)
    xn = x * lax.rsqrt(ms + EPS) * g1_ref[...]
    h = (xn.reshape(nb, tt, D_MODEL) * (1.0 + sc1) + sh1).reshape(n, D_MODEL)
    hb = h.astype(BF16)

    pre = _dot(hb, wqkv_ref[...]).reshape(nb, tt, CONV_CH)
    buf_ref[:, SUBLANES:SUBLANES + tt, :] = pre
    cw = cw_ref[...]
    conv = buf_ref[:, 5:5 + tt, :] * cw[0:1, :].reshape(1, 1, CONV_CH)
    conv = conv + buf_ref[:, 6:6 + tt, :] * cw[1:2, :].reshape(1, 1, CONV_CH)
    conv = conv + buf_ref[:, 7:7 + tt, :] * cw[2:3, :].reshape(1, 1, CONV_CH)
    conv = conv + pre * cw[3:4, :].reshape(1, 1, CONV_CH)
    conv_ref[...] = buf_ref[:, tt + 5:tt + SUBLANES, :]
    buf_ref[:, 0:SUBLANES, :] = buf_ref[:, tt:tt + SUBLANES, :]
    act = _silu(conv).reshape(n, CONV_CH)

    for hh in range(A_HEADS):
        lo = hh * A_DK
        qs = act[:, QA + lo:QA + lo + A_DK]
        ks = act[:, KA + lo:KA + lo + A_DK]
        qn = qs * lax.rsqrt(jnp.sum(qs * qs, axis=-1, keepdims=True) + EPS) * (A_DK ** -0.5)
        kn = ks * lax.rsqrt(jnp.sum(ks * ks, axis=-1, keepdims=True) + EPS)
        qa_ref[:, :, lo:lo + A_DK] = qn.astype(BF16).reshape(nb, tt, A_DK)
        ka_ref[:, :, lo:lo + A_DK] = kn.astype(BF16).reshape(nb, tt, A_DK)
    va_ref[...] = act[:, VA:GA].astype(BF16).reshape(nb, tt, A_WIDTH)

    gate = _dot(hb, wgate_ref[...])
    sg_ref[...] = _silu(gate).astype(BF16).reshape(nb, tt, A_WIDTH)

    ba = _dot(hb, wba_ref[...])
    lane = lax.broadcasted_iota(jnp.int32, (n, LANES), 1)
    beta = _sigmoid(ba)
    gdec = -jnp.exp(alog_ref[...]) * _softplus(ba + dtb_ref[...])
    aux = jnp.where(lane < A_HEADS, beta, jnp.where(lane < 2 * A_HEADS, gdec, 0.0))
    aux_ref[...] = aux.reshape(nb, tt, LANES)

    seg = seg_ref[...]
    pq = _dot(hb, wqb_ref[...])
    ssq = _dot((pq * pq).astype(BF16), seg)
    qb = pq * lax.rsqrt(ssq * (1.0 / B_DH) + EPS) * qng_ref[...] * (B_DH ** -0.5 * LOG2E)
    qb_ref[...] = qb.astype(BF16).reshape(nb, tt, B_WIDTH)
    pk = _dot(hb, wkb_ref[...])
    ssk = _dot((pk * pk).astype(BF16), seg)
    kb = pk * lax.rsqrt(ssk * (1.0 / B_DH) + EPS) * kng_ref[...]
    kb_ref[...] = kb.astype(BF16).reshape(nb, tt, B_WIDTH)
    vb = _dot(hb, wvb_ref[...])
    vb_ref[...] = vb.astype(BF16).reshape(nb, tt, B_WIDTH)

    @pl.when(t >= nt - nkeep)
    def _():
        kb32_ref[...] = kb.reshape(nb, tt, B_WIDTH)
        vb32_ref[...] = vb.reshape(nb, tt, B_WIDTH)


def _inproj(x, mod, left, wts, *, nb, tt):
    bsz, seq, _ = x.shape
    nt = seq // tt
    keep = min(BAND_PAST, seq)
    nkeep = keep // tt
    assert seq % tt == 0 and bsz % nb == 0 and keep % tt == 0 and tt % SUBLANES == 0
    grid = (bsz // nb, nt)
    tok = lambda w: pl.BlockSpec((nb, tt, w), lambda b, t: (b, t, 0))
    per_b = lambda r, w: pl.BlockSpec((nb, r, w), lambda b, t: (b, 0, 0))
    keep_spec = pl.BlockSpec((nb, tt, B_WIDTH), lambda b, t: (b, jnp.maximum(t - (nt - nkeep), 0), 0))
    bf = lambda w: jax.ShapeDtypeStruct((bsz, seq, w), BF16)
    out_shape = (bf(A_WIDTH), bf(A_WIDTH), bf(A_WIDTH), bf(A_WIDTH),
                 jax.ShapeDtypeStruct((bsz, seq, LANES), F32),
                 bf(B_WIDTH), bf(B_WIDTH), bf(B_WIDTH),
                 jax.ShapeDtypeStruct((bsz, keep, B_WIDTH), F32),
                 jax.ShapeDtypeStruct((bsz, keep, B_WIDTH), F32),
                 jax.ShapeDtypeStruct((bsz, CONV_W - 1, CONV_CH), F32))
    out_specs = (tok(A_WIDTH), tok(A_WIDTH), tok(A_WIDTH), tok(A_WIDTH), tok(LANES),
                 tok(B_WIDTH), tok(B_WIDTH), tok(B_WIDTH), keep_spec, keep_spec,
                 per_b(CONV_W - 1, CONV_CH))
    in_specs = [tok(D_MODEL), per_b(6, D_MODEL), per_b(CONV_W - 1, CONV_CH), _const_spec((1, D_MODEL)),
                _const_spec((D_MODEL, CONV_CH)), _const_spec((D_MODEL, A_WIDTH)), _const_spec((D_MODEL, LANES)),
                _const_spec((D_MODEL, B_WIDTH)), _const_spec((D_MODEL, B_WIDTH)), _const_spec((D_MODEL, B_WIDTH)),
                _const_spec((CONV_W, CONV_CH)), _const_spec((1, LANES)), _const_spec((1, LANES)),
                _const_spec((1, B_WIDTH)), _const_spec((1, B_WIDTH)), _const_spec((B_WIDTH, B_WIDTH))]
    return pl.pallas_call(
        functools.partial(_inproj_kernel, nb=nb, tt=tt, nt=nt, nkeep=nkeep),
        grid=grid, in_specs=in_specs, out_specs=out_specs, out_shape=out_shape,
        scratch_shapes=[pltpu.VMEM((nb, tt + SUBLANES, CONV_CH), F32)],
        compiler_params=pltpu.CompilerParams(dimension_semantics=("arbitrary", "arbitrary"),
                                             vmem_limit_bytes=VMEM_LIMIT),
        name="inproj",
    )(x, mod, left, wts["g1"], wts["wqkv"], wts["wgate"], wts["wba"], wts["wqb"], wts["wkb"], wts["wvb"],
      wts["cw"], wts["alog"], wts["dtb"], wts["qng"], wts["kng"], wts["seg"])


def _unit_lower_inverse(m, size):
    ri = lax.broadcasted_iota(jnp.int32, (size, size), 0)
    ci = lax.broadcasted_iota(jnp.int32, (size, size), 1)
    s = min(INV_BLOCK, size)
    sh = s.bit_length() - 1
    nd = jnp.where((ri >> sh) == (ci >> sh), -m, 0.0)
    x = jnp.where(ri == ci, 1.0, 0.0) + nd
    p = nd
    for _ in range(sh - 1):
        p = _dot_hi(p, p)
        x = x + _dot_hi(x, p)
    while s < size:
        same = (ri >> (sh + 1)) == (ci >> (sh + 1))
        low_left = (((ri >> sh) & 1) == 1) & (((ci >> sh) & 1) == 0)
        c = jnp.where(same & low_left, m, 0.0)
        x = x - _dot_hi(_dot_hi(x, c), x)
        s *= 2
        sh += 1
    return x


def _gdn_kernel(qa_ref, ka_ref, va_ref, sg_ref, aux_ref, s0_ref, ng_ref, oa_ref, sout_ref, s_ref, *, L, nc):
    c = pl.program_id(1)

    @pl.when(c == 0)
    def _():
        s_ref[...] = s0_ref[0]

    ri = lax.broadcasted_iota(jnp.int32, (L, L), 0)
    ci = lax.broadcasted_iota(jnp.int32, (L, L), 1)
    incl = ri >= ci
    strict = ri > ci
    aux = aux_ref[0]
    gc = _dot_hi(jnp.where(incl, 1.0, 0.0), aux)
    glast = gc[L - 1:L, :]
    eg = jnp.exp(gc)
    egl = jnp.exp(glast - gc)
    elast = jnp.exp(glast)
    gct = gc.T
    ng = ng_ref[...]

    for hh in range(A_HEADS):
        lo = hh * A_DK
        q = qa_ref[0, :, lo:lo + A_DK].astype(F32)
        k = ka_ref[0, :, lo:lo + A_DK].astype(F32)
        v = va_ref[0, :, lo:lo + A_DV].astype(F32)
        beta_c = aux[:, hh:hh + 1]
        gc_c = gc[:, A_HEADS + hh:A_HEADS + hh + 1]
        eg_c = eg[:, A_HEADS + hh:A_HEADS + hh + 1]
        egl_c = egl[:, A_HEADS + hh:A_HEADS + hh + 1]
        el = elast[:, A_HEADS + hh:A_HEADS + hh + 1]
        gc_r = gct[A_HEADS + hh:A_HEADS + hh + 1, :]
        decay = jnp.exp(jnp.where(incl, gc_c - gc_r, -jnp.inf))
        kb16 = k.astype(BF16)
        qkk = _dot_nt(jnp.concatenate([q, k], axis=0).astype(BF16), kb16)
        qk = qkk[0:L] * decay
        m = jnp.where(strict, qkk[L:2 * L] * beta_c * decay, 0.0)
        tinv = _unit_lower_inverse(m, L)
        rhs = jnp.concatenate([v * beta_c, k * (beta_c * eg_c)], axis=1)
        uw = _dot_hi(tinv, rhs)
        u = uw[:, 0:A_DV]
        w = uw[:, A_DV:A_DV + A_DK]
        s_h = s_ref[hh]
        res = _dot(jnp.concatenate([w, q * eg_c], axis=0).astype(BF16), s_h.astype(BF16))
        vnew = u - res[0:L]
        vnew16 = vnew.astype(BF16)
        o = res[L:2 * L] + _dot(qk.astype(BF16), vnew16)
        s_ref[hh] = s_h * el + _dot_tn((k * egl_c).astype(BF16), vnew16)
        on = o * lax.rsqrt(jnp.mean(o * o, axis=-1, keepdims=True) + EPS) * ng
        sg = sg_ref[0, :, lo:lo + A_DV].astype(F32)
        oa_ref[0, :, lo:lo + A_DV] = (on * sg).astype(BF16)

    @pl.when(c == nc - 1)
    def _():
        sout_ref[0] = s_ref[...]


def _gdn(qa, ka, va, sg, aux, s0, ng, *, L):
    bsz, seq, _ = qa.shape
    nc = seq // L
    assert seq % L == 0
    tok = lambda w: pl.BlockSpec((1, L, w), lambda b, c: (b, c, 0))
    st = pl.BlockSpec((1, A_HEADS, A_DK, A_DV), lambda b, c: (b, 0, 0, 0))
    return pl.pallas_call(
        functools.partial(_gdn_kernel, L=L, nc=nc),
        grid=(bsz, nc),
        in_specs=[tok(A_WIDTH), tok(A_WIDTH), tok(A_WIDTH), tok(A_WIDTH), tok(LANES), st, _const_spec((1, A_DV))],
        out_specs=(tok(A_WIDTH), st),
        out_shape=(jax.ShapeDtypeStruct((bsz, seq, A_WIDTH), BF16),
                   jax.ShapeDtypeStruct((bsz, A_HEADS, A_DK, A_DV), F32)),
        scratch_shapes=[pltpu.VMEM((A_HEADS, A_DK, A_DV), F32)],
        compiler_params=pltpu.CompilerParams(dimension_semantics=("arbitrary", "arbitrary"),
                                             vmem_limit_bytes=VMEM_LIMIT),
        name="gdn",
    )(qa, ka, va, sg, aux, s0, ng)


def _pair_blockdiag(p, lo):
    z = jnp.zeros_like(p)
    return jnp.concatenate([jnp.where(lo, p, z), jnp.where(lo, z, p)], axis=0)


def _pair_matmul(a, b, lo):
    return _dot(a.astype(BF16), _pair_blockdiag(b, lo).astype(BF16))


def _pair_unit_lower_inverse(ms, ri, cl, lo, tick):
    sh = INV_BLOCK.bit_length() - 1
    nds = [jnp.where((ri >> sh) == (cl >> sh), -m, 0.0) for m in ms]
    xs = [jnp.where(ri == cl, 1.0, 0.0) + nd for nd in nds]
    ps = [_pair_matmul(nd, nd, lo) for nd in nds]
    tick()
    for lvl in range(sh - 1):
        xs = [x + _pair_matmul(x, p, lo) for x, p in zip(xs, ps)]
        if lvl < sh - 2:
            ps = [_pair_matmul(p, p, lo) for p in ps]
        tick()
    s = INV_BLOCK
    while s < CHUNK:
        same = (ri >> (sh + 1)) == (cl >> (sh + 1))
        low_left = (((ri >> sh) & 1) == 1) & (((cl >> sh) & 1) == 0)
        ys = [_pair_matmul(x, jnp.where(same & low_left, m, 0.0), lo) for x, m in zip(xs, ms)]
        tick()
        xs = [x - _pair_matmul(y, x, lo) for x, y in zip(xs, ys)]
        tick()
        s *= 2
        sh += 1
    return xs


def _gdn_prepare(chunks, between):
    L = CHUNK
    pending = list(between)

    def tick():
        if pending:
            pending.pop(0)()

    ri = lax.broadcasted_iota(jnp.int32, (L, LANES), 0)
    lane = lax.broadcasted_iota(jnp.int32, (L, LANES), 1)
    cl = lane & (L - 1)
    lo = lane < L
    lo2 = lax.broadcasted_iota(jnp.int32, (2 * L, LANES), 1) < L
    tri = jnp.where(lax.broadcasted_iota(jnp.int32, (L, L), 0) >= lax.broadcasted_iota(jnp.int32, (L, L), 1), 1.0, 0.0)
    col = lambda x, j: x[:, j:j + 1]
    chains = [(g, pp) for g in range(len(chunks)) for pp in range(A_HEADS // 2)]
    hs_of = lambda pp: (2 * pp, 2 * pp + 1)
    auxs = [c[3] for c in chunks]

    gcs = [_dot_hi(tri, a) for a in auxs]
    g_pairs = [jnp.where(lo, col(auxs[g], A_HEADS + hs_of(pp)[0]), col(auxs[g], A_HEADS + hs_of(pp)[1]))
               for g, pp in chains]
    gdiffs = [_dot_hi(tri, jnp.where(ri > cl, gp, 0.0)) for gp in g_pairs]
    tick()
    qhs = [[chunks[g][0][:, h * A_DK:(h + 1) * A_DK] for h in hs_of(pp)] for g, pp in chains]
    khs = [[chunks[g][1][:, h * A_DK:(h + 1) * A_DK] for h in hs_of(pp)] for g, pp in chains]
    vhs = [[chunks[g][2][:, h * A_DV:(h + 1) * A_DV] for h in hs_of(pp)] for g, pp in chains]
    ksts = [jnp.concatenate(kh, axis=0) for kh in khs]
    r0s = [_dot_nt(jnp.concatenate([qh[0], kh[0]], axis=0), kst) for qh, kh, kst in zip(qhs, khs, ksts)]
    r1s = [_dot_nt(jnp.concatenate([qh[1], kh[1]], axis=0), kst) for qh, kh, kst in zip(qhs, khs, ksts)]
    tick()
    glasts = [gc[L - 1:L, :] for gc in gcs]
    egs = [jnp.exp(gc) for gc in gcs]
    egls = [jnp.exp(gl - gc) for gl, gc in zip(glasts, gcs)]
    decays = [jnp.where(ri >= cl, jnp.exp(gd), 0.0) for gd in gdiffs]
    grams = [jnp.where(lo2, r0, r1) for r0, r1 in zip(r0s, r1s)]
    qks = [gram[0:L] * dec for gram, dec in zip(grams, decays)]
    beta_pairs = [jnp.where(lo, col(auxs[g], hs_of(pp)[0]), col(auxs[g], hs_of(pp)[1])) for g, pp in chains]
    ms = [jnp.where(ri > cl, gram[L:2 * L] * bp * dec, 0.0) for gram, bp, dec in zip(grams, beta_pairs, decays)]
    tinvs = _pair_unit_lower_inverse(ms, ri, cl, lo, tick)

    uws, kdts = [], []
    for (g, pp), kh, vh, tinv in zip(chains, khs, vhs, tinvs):
        aux, eg, egl = auxs[g], egs[g], egls[g]
        kf = [x.astype(F32) for x in kh]
        rhs = jnp.concatenate(
            [jnp.concatenate([vh[e].astype(F32) * col(aux, h), kf[e] * (col(aux, h) * col(eg, A_HEADS + h))], axis=1)
             for e, h in enumerate(hs_of(pp))], axis=0)
        z = jnp.zeros_like(tinv)
        tsel = jnp.concatenate([jnp.where(lo, tinv, z), jnp.where(lo, z, tinv)], axis=0)
        uws.append(_dot(tsel.astype(BF16), rhs.astype(BF16)).astype(BF16))
        kdts.append(jnp.concatenate([kf[e] * col(egl, A_HEADS + h) for e, h in enumerate(hs_of(pp))], axis=0).T)
    tick()
    r6s = []
    for qk, kdt, uw in zip(qks, kdts, uws):
        zq = jnp.zeros_like(qk)
        zk = jnp.zeros_like(kdt)
        lhs = jnp.concatenate([jnp.where(lo, qk, zq), jnp.where(lo2, kdt, zk),
                               jnp.where(lo, zq, qk), jnp.where(lo2, zk, kdt)], axis=0)
        r6s.append(_dot(lhs.astype(BF16), uw))
    while pending:
        tick()
    out = [([None] * A_HEADS, jnp.exp(gl)) for gl in glasts]
    for (g, pp), qh, r6 in zip(chains, qhs, r6s):
        for e, h in enumerate(hs_of(pp)):
            base = e * (L + A_DK)
            o0 = r6[base:base + L, 0:A_DV]
            qkw = r6[base:base + L, A_DV:A_DV + A_DK]
            bm = r6[base + L:base + L + A_DK, 0:A_DV]
            cm = r6[base + L:base + L + A_DK, A_DV:A_DV + A_DK]
            qt = qh[e].astype(F32) * col(egs[g], A_HEADS + h) - qkw
            out[g][0][h] = (jnp.concatenate([cm, qt], axis=0).astype(BF16), bm, o0)
    return out


def _gdn_pipe_kernel(qa_ref, ka_ref, va_ref, aux_ref, sg_ref, s0_ref, ng_ref, oa_ref, sout_ref,
                     s_ref, cq_ref, b_ref, o0_ref, el_ref, *, nsteps):
    i = pl.program_id(0)
    L = CHUNK
    wr = i % 2
    rd = (i + 1) % 2

    @pl.when(i == 0)
    def _():
        s_ref[...] = jnp.zeros(s_ref.shape, F32)
        cq_ref[1] = jnp.zeros(cq_ref.shape[1:], BF16)
        b_ref[1] = jnp.zeros(b_ref.shape[1:], F32)
        o0_ref[1] = jnp.zeros(o0_ref.shape[1:], F32)
        el_ref[1] = jnp.zeros(el_ref.shape[1:], F32)

    j = jnp.maximum(i - 1, 0)
    first = (j % nsteps) == 0
    ng = ng_ref[...]
    state = [jnp.where(first, s0_ref[0, h], s_ref[h]) for h in range(A_HEADS)]

    def recur(g):
        def run():
            el = el_ref[rd, g, 0:1, :]
            rs = [_dot(cq_ref[rd, g, h], state[h].astype(BF16)) for h in range(A_HEADS)]
            for h in range(A_HEADS):
                o = rs[h][A_DK:A_DK + L] + o0_ref[rd, g, h]
                state[h] = state[h] * el[:, A_HEADS + h:A_HEADS + h + 1] - rs[h][0:A_DK] + b_ref[rd, g, h]
                on = o * lax.rsqrt(jnp.mean(o * o, axis=-1, keepdims=True) + EPS) * ng
                sg = sg_ref[0, g * L:(g + 1) * L, h * A_DV:(h + 1) * A_DV].astype(F32)
                oa_ref[0, g * L:(g + 1) * L, h * A_DV:(h + 1) * A_DV] = (on * sg).astype(BF16)
        return run

    chunks = [(qa_ref[0, g * L:(g + 1) * L, :], ka_ref[0, g * L:(g + 1) * L, :], va_ref[0, g * L:(g + 1) * L, :],
               aux_ref[0, g * L:(g + 1) * L, :]) for g in range(GDN_G)]
    prepared = _gdn_prepare(chunks, [recur(g) for g in range(GDN_G)])

    for h in range(A_HEADS):
        s_ref[h] = state[h]

    @pl.when((i >= 1) & ((j % nsteps) == nsteps - 1))
    def _():
        for h in range(A_HEADS):
            sout_ref[0, h] = state[h]

    for g, (heads, el) in enumerate(prepared):
        el_ref[wr, g] = jnp.broadcast_to(el, (SUBLANES, LANES))
        for h in range(A_HEADS):
            cq_ref[wr, g, h] = heads[h][0]
            b_ref[wr, g, h] = heads[h][1]
            o0_ref[wr, g, h] = heads[h][2]


def _gdn_pipe(qa, ka, va, sg, aux, s0, ng):
    bsz, seq, _ = qa.shape
    rows = GDN_G * CHUNK
    assert seq % rows == 0
    nsteps = seq // rows
    total = bsz * nsteps
    cur = lambda w: pl.BlockSpec((1, rows, w), lambda i: (jnp.minimum(i, total - 1) // nsteps,
                                                          jnp.minimum(i, total - 1) % nsteps, 0))
    prev = lambda w: pl.BlockSpec((1, rows, w), lambda i: (jnp.maximum(i - 1, 0) // nsteps,
                                                           jnp.maximum(i - 1, 0) % nsteps, 0))
    st = pl.BlockSpec((1, A_HEADS, A_DK, A_DV), lambda i: (jnp.maximum(i - 1, 0) // nsteps, 0, 0, 0))
    return pl.pallas_call(
        functools.partial(_gdn_pipe_kernel, nsteps=nsteps),
        grid=(total + 1,),
        in_specs=[cur(A_WIDTH), cur(A_WIDTH), cur(A_WIDTH), cur(LANES), prev(A_WIDTH), st, _const_spec((1, A_DV))],
        out_specs=(prev(A_WIDTH), st),
        out_shape=(jax.ShapeDtypeStruct((bsz, seq, A_WIDTH), BF16),
                   jax.ShapeDtypeStruct((bsz, A_HEADS, A_DK, A_DV), F32)),
        scratch_shapes=[pltpu.VMEM((A_HEADS, A_DK, A_DV), F32),
                        pltpu.VMEM((2, GDN_G, A_HEADS, A_DK + CHUNK, A_DV), BF16),
                        pltpu.VMEM((2, GDN_G, A_HEADS, A_DK, A_DV), F32),
                        pltpu.VMEM((2, GDN_G, A_HEADS, CHUNK, A_DV), F32),
                        pltpu.VMEM((2, GDN_G, SUBLANES, LANES), F32)],
        compiler_params=pltpu.CompilerParams(dimension_semantics=("arbitrary",), vmem_limit_bytes=VMEM_LIMIT),
        name="gdn_pipe",
    )(qa, ka, va, aux, sg, s0, ng)


def _relbias_kernel(rb_ref, o_ref):
    r = lax.broadcasted_iota(jnp.int32, (REL_SIZE, BIAS_EXT), 0)
    p = lax.broadcasted_iota(jnp.int32, (REL_SIZE, BIAS_EXT), 1)
    moff = jnp.where(p < WIN, p, p - BIAS_EXT)
    idx = jnp.clip(BAND_PAST - moff, -(CHUNK - 1), REL_MAX) + (CHUNK - 1)
    base = _dot_hi(rb_ref[...], jnp.where(r == idx, 1.0, 0.0)) * LOG2E
    qi = lax.broadcasted_iota(jnp.int32, (QBLK, WIN), 0) >> CHUNK_SHIFT
    kj = lax.broadcasted_iota(jnp.int32, (QBLK, WIN), 1) >> CHUNK_SHIFT
    in_band = (kj >= qi) & (kj <= qi + BAND_CHUNKS)
    for hh in range(B_HEADS):
        rows = jnp.broadcast_to(base[hh:hh + 1, :], (QBLK, BIAS_EXT))
        toep = pltpu.roll(rows, 0, 1, stride=1, stride_axis=0)
        o_ref[hh] = jnp.where(in_band, toep[:, 0:WIN], MASKED)


def _relbias(rel_bias):
    return pl.pallas_call(
        _relbias_kernel,
        in_specs=[pl.BlockSpec((B_HEADS, REL_SIZE), lambda: (0, 0))],
        out_specs=pl.BlockSpec((B_HEADS, QBLK, WIN), lambda: (0, 0, 0)),
        out_shape=jax.ShapeDtypeStruct((B_HEADS, QBLK, WIN), F32),
        compiler_params=pltpu.CompilerParams(vmem_limit_bytes=VMEM_LIMIT),
        name="relbias",
    )(rel_bias)


def _attn_prompt_kernel(q_ref, k_ref, v_ref, bias_ref, o_ref, kaug_ref, vaug_ref, *, seq):
    c = pl.program_id(1)

    @pl.when(c == 0)
    def _():
        lane = lax.broadcasted_iota(jnp.int32, (seq, B_WIDTH), 1)
        even = ((lane >> B_DH_SHIFT) & 1) == 0
        k = k_ref[0]
        v = v_ref[0]
        for e in range(2):
            mine = even if e == 0 else jnp.logical_not(even)
            kaug_ref[e, 0:BAND_PAST, :] = jnp.zeros((BAND_PAST, B_WIDTH), BF16)
            vaug_ref[e, 0:BAND_PAST, :] = jnp.zeros((BAND_PAST, B_WIDTH), BF16)
            kaug_ref[e, BAND_PAST:BAND_PAST + seq, :] = jnp.where(mine, k, jnp.zeros_like(k))
            vaug_ref[e, BAND_PAST:BAND_PAST + seq, :] = jnp.where(mine, v, jnp.ones_like(v))

    start = pl.multiple_of(c * QBLK, QBLK)
    lane = lax.broadcasted_iota(jnp.int32, (QBLK, LANES), 1)
    def scores(hh):
        lo = (hh // 2) * LANES
        kk = kaug_ref[hh % 2, pl.ds(start, WIN), lo:lo + LANES]
        return _dot_nt(q_ref[0, :, lo:lo + LANES], kk) + bias_ref[hh]

    res = []
    s_next = scores(0)
    for hh in range(B_HEADS):
        s = s_next
        if hh + 1 < B_HEADS:
            s_next = scores(hh + 1)
        lo = (hh // 2) * LANES
        pr = jnp.exp2(s - jnp.max(s, axis=-1, keepdims=True)).astype(BF16)
        vv = vaug_ref[hh % 2, pl.ds(start, WIN), lo:lo + LANES]
        res.append(_dot(pr, vv))
    for pp in range(B_HEADS // 2):
        lo = pp * LANES
        num = jnp.where(lane < B_DH, res[2 * pp], res[2 * pp + 1])
        den = pltpu.roll(jnp.where(lane < B_DH, res[2 * pp + 1], res[2 * pp]), B_DH, 1)
        o_ref[0, :, lo:lo + LANES] = (num / den).astype(BF16)


def _attn_prompt(qb, kb, vb, bias):
    bsz, seq, _ = qb.shape
    assert seq % QBLK == 0
    tok = pl.BlockSpec((1, QBLK, B_WIDTH), lambda b, c: (b, c, 0))
    full = pl.BlockSpec((1, seq, B_WIDTH), lambda b, c: (b, 0, 0))
    bias_spec = pl.BlockSpec((B_HEADS, QBLK, WIN), lambda b, c: (0, 0, 0), pipeline_mode=pl.Buffered(1))
    return pl.pallas_call(
        functools.partial(_attn_prompt_kernel, seq=seq),
        grid=(bsz, seq // QBLK),
        in_specs=[tok, full, full, bias_spec],
        out_specs=tok,
        out_shape=jax.ShapeDtypeStruct((bsz, seq, B_WIDTH), BF16),
        scratch_shapes=[pltpu.VMEM((2, BAND_PAST + seq, B_WIDTH), BF16),
                        pltpu.VMEM((2, BAND_PAST + seq, B_WIDTH), BF16)],
        compiler_params=pltpu.CompilerParams(dimension_semantics=("arbitrary", "arbitrary"),
                                             vmem_limit_bytes=VMEM_LIMIT),
        name="attn_prompt",
    )(qb, kb, vb, bias)


def _attn_sample_kernel(q_ref, kn_ref, vn_ref, kc_ref, vc_ref, bias_ref, o_ref, *, lc, tq):
    kc = kc_ref[0].astype(BF16)
    vc = vc_ref[0].astype(BF16)
    off = BAND_PAST - lc
    lane = lax.broadcasted_iota(jnp.int32, (tq, LANES), 1)
    for pp in range(B_HEADS // 2):
        lo = pp * LANES
        qp = q_ref[0, :, lo:lo + LANES]
        keys = [kc[:, lo:lo + LANES], kn_ref[0, :, lo:lo + LANES]]
        vals = [vc[:, lo:lo + LANES], vn_ref[0, :, lo:lo + LANES]]
        out = jnp.zeros((tq, LANES), F32)
        for e in range(2):
            hh = 2 * pp + e
            mine = (lane >= e * B_DH) & (lane < (e + 1) * B_DH)
            qm = jnp.where(mine, qp, jnp.zeros_like(qp))
            ss = [_dot_nt(qm, keys[0]) + bias_ref[hh, 0:tq, off:off + lc],
                  _dot_nt(qm, keys[1]) + bias_ref[hh, 0:tq, BAND_PAST:BAND_PAST + tq]]
            mx = jnp.maximum(jnp.max(ss[0], axis=-1, keepdims=True), jnp.max(ss[1], axis=-1, keepdims=True))
            es = [jnp.exp2(s - mx) for s in ss]
            den = jnp.sum(es[0], axis=-1, keepdims=True) + jnp.sum(es[1], axis=-1, keepdims=True)
            acc = _dot((es[0] / den).astype(BF16), vals[0]) + _dot((es[1] / den).astype(BF16), vals[1])
            out = out + jnp.where(mine, acc, 0.0)
        o_ref[0, :, lo:lo + LANES] = out.astype(BF16)


def _attn_sample(qb, kb, vb, k_cache, v_cache, bias):
    bsz, tq, _ = qb.shape
    lc = k_cache.shape[1]
    assert lc <= BAND_PAST and tq <= CHUNK
    tok = pl.BlockSpec((1, tq, B_WIDTH), lambda b: (b, 0, 0))
    cache = pl.BlockSpec((1, lc, B_WIDTH), lambda b: (b, 0, 0))
    bias_spec = pl.BlockSpec((B_HEADS, QBLK, WIN), lambda b: (0, 0, 0), pipeline_mode=pl.Buffered(1))
    return pl.pallas_call(
        functools.partial(_attn_sample_kernel, lc=lc, tq=tq),
        grid=(bsz,),
        in_specs=[tok, tok, tok, cache, cache, bias_spec],
        out_specs=tok,
        out_shape=jax.ShapeDtypeStruct((bsz, tq, B_WIDTH), BF16),
        compiler_params=pltpu.CompilerParams(dimension_semantics=("arbitrary",), vmem_limit_bytes=VMEM_LIMIT),
        name="attn_sample",
    )(qb, kb, vb, k_cache.reshape(bsz, lc, B_WIDTH), v_cache.reshape(bsz, lc, B_WIDTH), bias)


def _outffn_kernel(x_ref, oa_ref, ob_ref, mod_ref, g2_ref, woa_ref, wob_ref, wup_ref, wdn_ref, y_ref, *, nb, tt):
    n = nb * tt
    mod = mod_ref[...]
    gate1 = mod[:, 2:3, :]
    sh2 = mod[:, 3:4, :]
    sc2 = mod[:, 4:5, :]
    gate2 = mod[:, 5:6, :]
    mix = _dot(oa_ref[...].reshape(n, A_WIDTH), woa_ref[...]) + _dot(ob_ref[...].reshape(n, B_WIDTH), wob_ref[...])
    y1 = x_ref[...] + gate1 * mix.reshape(nb, tt, D_MODEL)
    y1f = y1.reshape(n, D_MODEL)
    ms = jnp.mean(y1f * y1f, axis=-1, keepdims=True)
    yn = y1f * lax.rsqrt(ms + EPS) * g2_ref[...]
    h2 = (yn.reshape(nb, tt, D_MODEL) * (1.0 + sc2) + sh2).reshape(n, D_MODEL).astype(BF16)
    acc = jnp.zeros((n, D_MODEL), F32)
    fc = D_MODEL
    for jj in range(D_FF // fc):
        u = _dot(h2, wup_ref[:, jj * fc:(jj + 1) * fc])
        r = jnp.maximum(u, 0.0)
        acc = acc + _dot((r * r).astype(BF16), wdn_ref[jj * fc:(jj + 1) * fc, :])
    y_ref[...] = y1 + gate2 * acc.reshape(nb, tt, D_MODEL)


def _outffn(x, oa, ob, mod, wts, *, nb, tt):
    bsz, seq, _ = x.shape
    assert seq % tt == 0 and bsz % nb == 0
    tok = lambda w: pl.BlockSpec((nb, tt, w), lambda b, t: (b, t, 0))
    once = lambda shape: pl.BlockSpec(shape, lambda b, t: (0,) * len(shape), pipeline_mode=pl.Buffered(1))
    return pl.pallas_call(
        functools.partial(_outffn_kernel, nb=nb, tt=tt),
        grid=(bsz // nb, seq // tt),
        in_specs=[tok(D_MODEL), tok(A_WIDTH), tok(B_WIDTH),
                  pl.BlockSpec((nb, 6, D_MODEL), lambda b, t: (b, 0, 0)), once((1, D_MODEL)),
                  once((A_WIDTH, D_MODEL)), once((B_WIDTH, D_MODEL)), once((D_MODEL, D_FF)), once((D_FF, D_MODEL))],
        out_specs=tok(D_MODEL),
        out_shape=jax.ShapeDtypeStruct((bsz, seq, D_MODEL), F32),
        compiler_params=pltpu.CompilerParams(dimension_semantics=("arbitrary", "arbitrary"),
                                             vmem_limit_bytes=VMEM_LIMIT),
        name="outffn",
    )(x, oa, ob, mod, wts["g2"], wts["woa"], wts["wob"], wts["wup"], wts["wdn"])


def _prep_weights(norm1_g, norm2_g, w_in, conv_w, a_log, dt_bias, gdn_norm_g, qn_g, kn_g, w_out, w_up, w_down):
    pad_ba = jnp.zeros((D_MODEL, LANES - 2 * A_HEADS), F32)
    pad_row = lambda v: jnp.zeros((1, LANES), F32).at[0, A_HEADS:2 * A_HEADS].set(v)
    head = jnp.arange(B_WIDTH) // B_DH
    return dict(
        g1=norm1_g.reshape(1, D_MODEL), g2=norm2_g.reshape(1, D_MODEL),
        wqkv=w_in[:, QA:GA].astype(BF16), wgate=w_in[:, GA:BA].astype(BF16),
        wba=jnp.concatenate([w_in[:, BA:QB], pad_ba], axis=1).astype(BF16),
        wqb=w_in[:, QB:KB].astype(BF16), wkb=w_in[:, KB:VB].astype(BF16), wvb=w_in[:, VB:IN_COLS].astype(BF16),
        cw=conv_w, alog=pad_row(a_log), dtb=pad_row(dt_bias), ng=gdn_norm_g.reshape(1, A_DV),
        qng=jnp.tile(qn_g, B_HEADS).reshape(1, B_WIDTH), kng=jnp.tile(kn_g, B_HEADS).reshape(1, B_WIDTH),
        seg=(head[:, None] == head[None, :]).astype(BF16),
        woa=w_out[:A_WIDTH].astype(BF16), wob=w_out[A_WIDTH:].astype(BF16),
        wup=w_up.astype(BF16), wdn=w_down.astype(BF16))


def _layer(x, mod, left, s0, cache, bias, wts, *, nb, tt, chunk):
    bsz, seq, _ = x.shape
    qa, ka, va, sg, aux, qb, kb, vb, kb32, vb32, conv_state = _inproj(x, mod, left, wts, nb=nb, tt=tt)
    if chunk == CHUNK and seq % (GDN_G * CHUNK) == 0:
        oa, s_new = _gdn_pipe(qa, ka, va, sg, aux, s0, wts["ng"])
    else:
        oa, s_new = _gdn(qa, ka, va, sg, aux, s0, wts["ng"], L=chunk)
    if cache is None:
        ob = _attn_prompt(qb, kb, vb, bias)
    else:
        ob = _attn_sample(qb, kb, vb, cache[0], cache[1], bias)
    y = _outffn(x, oa, ob, mod, wts, nb=nb, tt=tt)
    keep = kb32.shape[1]
    return (y, conv_state, s_new, kb32.reshape(bsz, keep, B_HEADS, B_DH), vb32.reshape(bsz, keep, B_HEADS, B_DH))


def kernel(x_prompt, x_sample, state_conv, state_gdn, cache_k_band, cache_v_band, c_prompt, c_sample,
           w_mod, b_mod, norm1_g, norm2_g, w_in, conv_w, a_log, dt_bias, gdn_norm_g, qn_g, kn_g,
           rel_bias, w_out, w_up, w_down):
    depth = w_mod.shape[0]
    bp, tp, _ = x_prompt.shape
    bs, ts, _ = x_sample.shape
    yp, ys = x_prompt, x_sample
    c_all = jnp.concatenate([c_prompt, c_sample], axis=0)
    outs = [[] for _ in range(8)]
    for l in range(depth):
        wts = _prep_weights(norm1_g[l], norm2_g[l], w_in[l], conv_w[l], a_log[l], dt_bias[l], gdn_norm_g[l],
                            qn_g[l], kn_g[l], w_out[l], w_up[l], w_down[l])
        mod = _modulation(c_all, w_mod[l], b_mod[l]).reshape(bp + bs, 6, D_MODEL)
        bias = _relbias(rel_bias[l])
        yp, cp, gp, kp, vp = _layer(
            yp, mod[:bp], jnp.zeros((bp, CONV_W - 1, CONV_CH), F32), jnp.zeros((bp, A_HEADS, A_DK, A_DV), F32),
            None, bias, wts, nb=1, tt=min(512, tp), chunk=min(CHUNK, tp))
        ys, cs, gs, ks, vs = _layer(
            ys, mod[bp:], state_conv[l], state_gdn[l], (cache_k_band[l], cache_v_band[l]), bias, wts,
            nb=bs, tt=ts, chunk=min(CHUNK, ts))
        for acc, val in zip(outs, (cp, gp, kp, vp, cs, gs, ks, vs)):
            acc.append(val)
    return (yp, ys) + tuple(jnp.stack(o) for o in outs)
```

```python
import functools

import jax
import jax.numpy as jnp
from jax import lax
from jax.experimental import pallas as pl
from jax.experimental.pallas import tpu as pltpu

D_MODEL = 1024
CHUNK = 64
A_HEADS = 4
A_DK = 128
A_DV = 128
A_WIDTH = A_HEADS * A_DV
CONV_W = 4
B_HEADS = 8
B_DH = 64
B_WIDTH = B_HEADS * B_DH
BAND_CHUNKS = 8
BAND_PAST = BAND_CHUNKS * CHUNK
BAND = BAND_PAST + CHUNK
REL_MAX = 4 * CHUNK
REL_SIZE = REL_MAX + CHUNK
D_FF = 4 * D_MODEL
EPS = 1e-6

QA = 0
KA = QA + A_HEADS * A_DK
VA = KA + A_HEADS * A_DK
GA = VA + A_WIDTH
BA = GA + A_WIDTH
AA = BA + A_HEADS
QB = AA + A_HEADS
KB = QB + B_WIDTH
VB = KB + B_WIDTH
IN_COLS = VB + B_WIDTH
CONV_CH = GA - QA

LANES = 128
SUBLANES = 8
INV_BLOCK = 16
GDN_G = 8
QBLK = 4 * CHUNK
WIN = BAND_PAST + QBLK
BIAS_EXT = 1024
CHUNK_SHIFT = CHUNK.bit_length() - 1
B_DH_SHIFT = B_DH.bit_length() - 1
LOG2E = 1.4426950408889634
MASKED = -1e30
VMEM_LIMIT = 56 * 1024 * 1024

F32 = jnp.float32
BF16 = jnp.bfloat16
HI = lax.Precision.HIGHEST


def _dot(a, b):
    return jnp.dot(a, b, preferred_element_type=F32)


def _dot_hi(a, b):
    return jnp.dot(a, b, preferred_element_type=F32, precision=HI)


def _dot_nt(a, b):
    return lax.dot_general(a, b, (((1,), (1,)), ((), ())), preferred_element_type=F32)


def _dot_tn(a, b):
    return lax.dot_general(a, b, (((0,), (0,)), ((), ())), preferred_element_type=F32)


def _sigmoid(x):
    return 1.0 / (1.0 + jnp.exp(-x))


def _silu(x):
    return x * _sigmoid(x)


def _softplus(x):
    return jnp.maximum(x, 0.0) + jnp.log(1.0 + jnp.exp(-jnp.abs(x)))


def _const_spec(shape):
    n = len(shape)
    return pl.BlockSpec(shape, lambda *_: (0,) * n)


def _mod_kernel(c_ref, w_ref, b_ref, o_ref):
    c = c_ref[...]
    o_ref[...] = _dot(_silu(c).astype(BF16), w_ref[...].astype(BF16)) + b_ref[...]


def _modulation(c, w_mod, b_mod):
    nb = c.shape[0]
    cols = w_mod.shape[1]
    bn = 512
    return pl.pallas_call(
        _mod_kernel,
        grid=(cols // bn,),
        in_specs=[pl.BlockSpec((nb, D_MODEL), lambda j: (0, 0)),
                  pl.BlockSpec((D_MODEL, bn), lambda j: (0, j)),
                  pl.BlockSpec((1, bn), lambda j: (0, j))],
        out_specs=pl.BlockSpec((nb, bn), lambda j: (0, j)),
        out_shape=jax.ShapeDtypeStruct((nb, cols), F32),
        compiler_params=pltpu.CompilerParams(dimension_semantics=("arbitrary",)),
        name="modulation",
    )(c, w_mod, b_mod.reshape(1, cols))


def _inproj_kernel(x_ref, mod_ref, left_ref, g1_ref, wqkv_ref, wgate_ref, wba_ref, wqb_ref, wkb_ref, wvb_ref,
                   cw_ref, alog_ref, dtb_ref, qng_ref, kng_ref, seg_ref,
                   qa_ref, ka_ref, va_ref, sg_ref, aux_ref, qb_ref, kb_ref, vb_ref, kb32_ref, vb32_ref, conv_ref,
                   buf_ref, *, nb, tt, nt, nkeep):
    t = pl.program_id(1)
    n = nb * tt

    @pl.when(t == 0)
    def _():
        buf_ref[:, 0:SUBLANES, :] = jnp.zeros((nb, SUBLANES, CONV_CH), F32)
        buf_ref[:, SUBLANES - (CONV_W - 1):SUBLANES, :] = left_ref[...]

    x = x_ref[...].reshape(n, D_MODEL)
    mod = mod_ref[...]
    sh1 = mod[:, 0:1, :]
    mul1 = g1_ref[...].reshape(1, 1, D_MODEL) * (1.0 + mod[:, 1:2, :])
    ms = jnp.mean(x * x, axis=-1, keepdims=True)
    xn = x * lax.rsqrt(ms + EPS)
    hb = (xn.reshape(nb, tt, D_MODEL) * mul1 + sh1).reshape(n, D_MODEL).astype(BF16)

    pre = _dot(hb, wqkv_ref[...]).reshape(nb, tt, CONV_CH)
    buf_ref[:, SUBLANES:SUBLANES + tt, :] = pre
    cw = cw_ref[...]
    full = buf_ref[...].reshape(nb * (tt + SUBLANES), CONV_CH)
    prev = pltpu.roll(full, 1, 0)
    pair_b = pltpu.roll(full * cw[1:2, :] + prev * cw[0:1, :], 2, 0)
    conv = (full * cw[3:4, :] + prev * cw[2:3, :] + pair_b).reshape(nb, tt + SUBLANES, CONV_CH)[:, SUBLANES:, :]
    conv_ref[...] = buf_ref[:, tt + 5:tt + SUBLANES, :]
    buf_ref[:, 0:SUBLANES, :] = buf_ref[:, tt:tt + SUBLANES, :]
    act = _silu(conv).reshape(n, CONV_CH)

    for hh in range(A_HEADS):
        lo = hh * A_DK
        qs = act[:, QA + lo:QA + lo + A_DK]
        ks = act[:, KA + lo:KA + lo + A_DK]
        qn = qs * lax.rsqrt(jnp.sum(qs * qs, axis=-1, keepdims=True) + EPS) * (A_DK ** -0.5)
        kn = ks * lax.rsqrt(jnp.sum(ks * ks, axis=-1, keepdims=True) + EPS)
        qa_ref[:, :, lo:lo + A_DK] = qn.astype(BF16).reshape(nb, tt, A_DK)
        ka_ref[:, :, lo:lo + A_DK] = kn.astype(BF16).reshape(nb, tt, A_DK)
    va_ref[...] = act[:, VA:GA].astype(BF16).reshape(nb, tt, A_WIDTH)

    gate = _dot(hb, wgate_ref[...])
    sg_ref[...] = _silu(gate).astype(BF16).reshape(nb, tt, A_WIDTH)

    ba = _dot(hb, wba_ref[...])
    lane = lax.broadcasted_iota(jnp.int32, (n, LANES), 1)
    beta = _sigmoid(ba)
    gdec = -jnp.exp(alog_ref[...]) * _softplus(ba + dtb_ref[...])
    aux = jnp.where(lane < A_HEADS, beta, jnp.where(lane < 2 * A_HEADS, gdec, 0.0))
    aux_ref[...] = aux.reshape(nb, tt, LANES)

    seg = seg_ref[...]
    pq = _dot(hb, wqb_ref[...])
    ssq = _dot((pq * pq).astype(BF16), seg)
    qb = pq * lax.rsqrt(ssq * (1.0 / B_DH) + EPS) * qng_ref[...] * (B_DH ** -0.5 * LOG2E)
    qb_ref[...] = qb.astype(BF16).reshape(nb, tt, B_WIDTH)
    pk = _dot(hb, wkb_ref[...])
    ssk = _dot((pk * pk).astype(BF16), seg)
    kb = pk * lax.rsqrt(ssk * (1.0 / B_DH) + EPS) * kng_ref[...]
    kb_ref[...] = kb.astype(BF16).reshape(nb, tt, B_WIDTH)
    vb = _dot(hb, wvb_ref[...])
    vb_ref[...] = vb.astype(BF16).reshape(nb, tt, B_WIDTH)

    @pl.when(t >= nt - nkeep)
    def _():
        kb32_ref[...] = kb.reshape(nb, tt, B_WIDTH)
        vb32_ref[...] = vb.reshape(nb, tt, B_WIDTH)


def _inproj(x, mod, left, wts, *, nb, tt):
    bsz, seq, _ = x.shape
    nt = seq // tt
    keep = min(BAND_PAST, seq)
    nkeep = keep // tt
    assert seq % tt == 0 and bsz % nb == 0 and keep % tt == 0 and tt % SUBLANES == 0
    grid = (bsz // nb, nt)
    tok = lambda w: pl.BlockSpec((nb, tt, w), lambda b, t: (b, t, 0))
    per_b = lambda r, w: pl.BlockSpec((nb, r, w), lambda b, t: (b, 0, 0))
    keep_spec = pl.BlockSpec((nb, tt, B_WIDTH), lambda b, t: (b, jnp.maximum(t - (nt - nkeep), 0), 0))
    bf = lambda w: jax.ShapeDtypeStruct((bsz, seq, w), BF16)
    out_shape = (bf(A_WIDTH), bf(A_WIDTH), bf(A_WIDTH), bf(A_WIDTH),
                 jax.ShapeDtypeStruct((bsz, seq, LANES), F32),
                 bf(B_WIDTH), bf(B_WIDTH), bf(B_WIDTH),
                 jax.ShapeDtypeStruct((bsz, keep, B_WIDTH), F32),
                 jax.ShapeDtypeStruct((bsz, keep, B_WIDTH), F32),
                 jax.ShapeDtypeStruct((bsz, CONV_W - 1, CONV_CH), F32))
    out_specs = (tok(A_WIDTH), tok(A_WIDTH), tok(A_WIDTH), tok(A_WIDTH), tok(LANES),
                 tok(B_WIDTH), tok(B_WIDTH), tok(B_WIDTH), keep_spec, keep_spec,
                 per_b(CONV_W - 1, CONV_CH))
    in_specs = [tok(D_MODEL), per_b(6, D_MODEL), per_b(CONV_W - 1, CONV_CH), _const_spec((1, D_MODEL)),
                _const_spec((D_MODEL, CONV_CH)), _const_spec((D_MODEL, A_WIDTH)), _const_spec((D_MODEL, LANES)),
                _const_spec((D_MODEL, B_WIDTH)), _const_spec((D_MODEL, B_WIDTH)), _const_spec((D_MODEL, B_WIDTH)),
                _const_spec((CONV_W, CONV_CH)), _const_spec((1, LANES)), _const_spec((1, LANES)),
                _const_spec((1, B_WIDTH)), _const_spec((1, B_WIDTH)), _const_spec((B_WIDTH, B_WIDTH))]
    return pl.pallas_call(
        functools.partial(_inproj_kernel, nb=nb, tt=tt, nt=nt, nkeep=nkeep),
        grid=grid, in_specs=in_specs, out_specs=out_specs, out_shape=out_shape,
        scratch_shapes=[pltpu.VMEM((nb, tt + SUBLANES, CONV_CH), F32)],
        compiler_params=pltpu.CompilerParams(dimension_semantics=("arbitrary", "arbitrary"),
                                             vmem_limit_bytes=VMEM_LIMIT),
        name="inproj",
    )(x, mod, left, wts["g1"], wts["wqkv"], wts["wgate"], wts["wba"], wts["wqb"], wts["wkb"], wts["wvb"],
      wts["cw"], wts["alog"], wts["dtb"], wts["qng"], wts["kng"], wts["seg"])


def _unit_lower_inverse(m, size):
    ri = lax.broadcasted_iota(jnp.int32, (size, size), 0)
    ci = lax.broadcasted_iota(jnp.int32, (size, size), 1)
    s = min(INV_BLOCK, size)
    sh = s.bit_length() - 1
    nd = jnp.where((ri >> sh) == (ci >> sh), -m, 0.0)
    x = jnp.where(ri == ci, 1.0, 0.0) + nd
    p = nd
    for _ in range(sh - 1):
        p = _dot_hi(p, p)
        x = x + _dot_hi(x, p)
    while s < size:
        same = (ri >> (sh + 1)) == (ci >> (sh + 1))
        low_left = (((ri >> sh) & 1) == 1) & (((ci >> sh) & 1) == 0)
        c = jnp.where(same & low_left, m, 0.0)
        x = x - _dot_hi(_dot_hi(x, c), x)
        s *= 2
        sh += 1
    return x


def _gdn_kernel(qa_ref, ka_ref, va_ref, sg_ref, aux_ref, s0_ref, ng_ref, oa_ref, sout_ref, s_ref, *, L, nc):
    c = pl.program_id(1)

    @pl.when(c == 0)
    def _():
        s_ref[...] = s0_ref[0]

    ri = lax.broadcasted_iota(jnp.int32, (L, L), 0)
    ci = lax.broadcasted_iota(jnp.int32, (L, L), 1)
    incl = ri >= ci
    strict = ri > ci
    aux = aux_ref[0]
    gc = _dot_hi(jnp.where(incl, 1.0, 0.0), aux)
    glast = gc[L - 1:L, :]
    eg = jnp.exp(gc)
    egl = jnp.exp(glast - gc)
    elast = jnp.exp(glast)
    gct = gc.T
    ng = ng_ref[...]

    for hh in range(A_HEADS):
        lo = hh * A_DK
        q = qa_ref[0, :, lo:lo + A_DK].astype(F32)
        k = ka_ref[0, :, lo:lo + A_DK].astype(F32)
        v = va_ref[0, :, lo:lo + A_DV].astype(F32)
        beta_c = aux[:, hh:hh + 1]
        gc_c = gc[:, A_HEADS + hh:A_HEADS + hh + 1]
        eg_c = eg[:, A_HEADS + hh:A_HEADS + hh + 1]
        egl_c = egl[:, A_HEADS + hh:A_HEADS + hh + 1]
        el = elast[:, A_HEADS + hh:A_HEADS + hh + 1]
        gc_r = gct[A_HEADS + hh:A_HEADS + hh + 1, :]
        decay = jnp.exp(jnp.where(incl, gc_c - gc_r, -jnp.inf))
        kb16 = k.astype(BF16)
        qkk = _dot_nt(jnp.concatenate([q, k], axis=0).astype(BF16), kb16)
        qk = qkk[0:L] * decay
        m = jnp.where(strict, qkk[L:2 * L] * beta_c * decay, 0.0)
        tinv = _unit_lower_inverse(m, L)
        rhs = jnp.concatenate([v * beta_c, k * (beta_c * eg_c)], axis=1)
        uw = _dot_hi(tinv, rhs)
        u = uw[:, 0:A_DV]
        w = uw[:, A_DV:A_DV + A_DK]
        s_h = s_ref[hh]
        res = _dot(jnp.concatenate([w, q * eg_c], axis=0).astype(BF16), s_h.astype(BF16))
        vnew = u - res[0:L]
        vnew16 = vnew.astype(BF16)
        o = res[L:2 * L] + _dot(qk.astype(BF16), vnew16)
        s_ref[hh] = s_h * el + _dot_tn((k * egl_c).astype(BF16), vnew16)
        on = o * lax.rsqrt(jnp.mean(o * o, axis=-1, keepdims=True) + EPS) * ng
        sg = sg_ref[0, :, lo:lo + A_DV].astype(F32)
        oa_ref[0, :, lo:lo + A_DV] = (on * sg).astype(BF16)

    @pl.when(c == nc - 1)
    def _():
        sout_ref[0] = s_ref[...]


def _gdn(qa, ka, va, sg, aux, s0, ng, *, L):
    bsz, seq, _ = qa.shape
    nc = seq // L
    assert seq % L == 0
    tok = lambda w: pl.BlockSpec((1, L, w), lambda b, c: (b, c, 0))
    st = pl.BlockSpec((1, A_HEADS, A_DK, A_DV), lambda b, c: (b, 0, 0, 0))
    return pl.pallas_call(
        functools.partial(_gdn_kernel, L=L, nc=nc),
        grid=(bsz, nc),
        in_specs=[tok(A_WIDTH), tok(A_WIDTH), tok(A_WIDTH), tok(A_WIDTH), tok(LANES), st, _const_spec((1, A_DV))],
        out_specs=(tok(A_WIDTH), st),
        out_shape=(jax.ShapeDtypeStruct((bsz, seq, A_WIDTH), BF16),
                   jax.ShapeDtypeStruct((bsz, A_HEADS, A_DK, A_DV), F32)),
        scratch_shapes=[pltpu.VMEM((A_HEADS, A_DK, A_DV), F32)],
        compiler_params=pltpu.CompilerParams(dimension_semantics=("arbitrary", "arbitrary"),
                                             vmem_limit_bytes=VMEM_LIMIT),
        name="gdn",
    )(qa, ka, va, sg, aux, s0, ng)


def _pair_blockdiag(p, lo):
    z = jnp.zeros_like(p)
    return jnp.concatenate([jnp.where(lo, p, z), jnp.where(lo, z, p)], axis=0)


def _pair_matmul(a, b, lo):
    return _dot(a, _pair_blockdiag(b, lo))


def _pair_unit_lower_inverse(ms, ri, cl, lo, tick):
    sh = INV_BLOCK.bit_length() - 1
    diag_blk = (ri >> sh) == (cl >> sh)
    xs = [jnp.where(ri == cl, 1.0, 0.0) - jnp.where(diag_blk, m, 0.0) for m in ms]
    m16s = [m.astype(BF16) for m in ms]
    zero16 = jnp.zeros_like(m16s[0])
    nds = [jnp.where(diag_blk, -m16, zero16) for m16 in m16s]
    ps = [_pair_matmul(nd, nd, lo).astype(BF16) for nd in nds]
    tick()
    for lvl in range(sh - 1):
        if lvl < sh - 2:
            rs = [_pair_matmul(jnp.concatenate([x.astype(BF16), p], axis=0), p, lo) for x, p in zip(xs, ps)]
            xs = [x + r[0:CHUNK] for x, r in zip(xs, rs)]
            ps = [r[CHUNK:2 * CHUNK].astype(BF16) for r in rs]
        else:
            xs = [x + _pair_matmul(x.astype(BF16), p, lo) for x, p in zip(xs, ps)]
        tick()
    s = INV_BLOCK
    while s < CHUNK:
        same = (ri >> (sh + 1)) == (cl >> (sh + 1))
        low_left = (((ri >> sh) & 1) == 1) & (((cl >> sh) & 1) == 0)
        x16s = [x.astype(BF16) for x in xs]
        ys = [_pair_matmul(x16, jnp.where(same & low_left, m16, zero16), lo).astype(BF16)
              for x16, m16 in zip(x16s, m16s)]
        tick()
        xs = [x - _pair_matmul(y, x16, lo) for x, y, x16 in zip(xs, ys, x16s)]
        tick()
        s *= 2
        sh += 1
    return xs


def _gdn_prepare(chunks, between):
    L = CHUNK
    pending = list(between)

    def tick():
        if pending:
            pending.pop(0)()

    ri = lax.broadcasted_iota(jnp.int32, (L, LANES), 0)
    lane = lax.broadcasted_iota(jnp.int32, (L, LANES), 1)
    cl = lane & (L - 1)
    lo = lane < L
    lo2 = lax.broadcasted_iota(jnp.int32, (2 * L, LANES), 1) < L
    zk16 = jnp.zeros((L, A_DK), BF16)
    tri = jnp.where(lax.broadcasted_iota(jnp.int32, (L, L), 0) >= lax.broadcasted_iota(jnp.int32, (L, L), 1), 1.0, 0.0)
    col = lambda x, j: x[:, j:j + 1]
    chains = [(g, pp) for g in range(len(chunks)) for pp in range(A_HEADS // 2)]
    hs_of = lambda pp: (2 * pp, 2 * pp + 1)
    auxs = [c[3] for c in chunks]

    g_pairs = [jnp.where(lo, col(auxs[g], A_HEADS + hs_of(pp)[0]), col(auxs[g], A_HEADS + hs_of(pp)[1]))
               for g, pp in chains]
    gdiffs = [_dot_hi(tri, jnp.where(ri > cl, gp, 0.0)) for gp in g_pairs]
    tick()
    qhs = [[chunks[g][0][:, h * A_DK:(h + 1) * A_DK] for h in hs_of(pp)] for g, pp in chains]
    khs = [[chunks[g][1][:, h * A_DK:(h + 1) * A_DK] for h in hs_of(pp)] for g, pp in chains]
    vhs = [[chunks[g][2][:, h * A_DV:(h + 1) * A_DV] for h in hs_of(pp)] for g, pp in chains]
    kbds = [jnp.concatenate([jnp.concatenate([kh[0], zk16], axis=1), jnp.concatenate([zk16, kh[1]], axis=1)], axis=0)
            for kh in khs]
    grams = [_dot_nt(jnp.concatenate([jnp.concatenate(qh, axis=1), jnp.concatenate(kh, axis=1)], axis=0), kbd)
             for qh, kh, kbd in zip(qhs, khs, kbds)]
    tick()
    gcs = [[gd[:, e * L:e * L + 1] + auxs[g][0:1, A_HEADS + h:A_HEADS + h + 1] for e, h in enumerate(hs_of(pp))]
           for (g, pp), gd in zip(chains, gdiffs)]
    glasts = [[gc[L - 1:L, :] for gc in pair] for pair in gcs]
    egs = [[jnp.exp(gc) for gc in pair] for pair in gcs]
    egls = [[jnp.exp(gl - gc) for gl, gc in zip(gls, pair)] for gls, pair in zip(glasts, gcs)]
    decays = [jnp.where(ri >= cl, jnp.exp(gd), 0.0) for gd in gdiffs]
    qks = [(gram[0:L] * dec).astype(BF16) for gram, dec in zip(grams, decays)]
    beta_pairs = [jnp.where(lo, col(auxs[g], hs_of(pp)[0]), col(auxs[g], hs_of(pp)[1])) for g, pp in chains]
    ms = [jnp.where(ri > cl, gram[L:2 * L] * bp * dec, 0.0) for gram, bp, dec in zip(grams, beta_pairs, decays)]
    tinvs = _pair_unit_lower_inverse(ms, ri, cl, lo, tick)

    uws, kdts = [], []
    for (g, pp), kh, vh, tinv, eg, egl in zip(chains, khs, vhs, tinvs, egs, egls):
        aux = auxs[g]
        kf = [x.astype(F32) for x in kh]
        rhs = jnp.concatenate(
            [jnp.concatenate([vh[e].astype(F32) * col(aux, h), kf[e] * (col(aux, h) * eg[e])], axis=1)
             for e, h in enumerate(hs_of(pp))], axis=0)
        z = jnp.zeros_like(tinv)
        tsel = jnp.concatenate([jnp.where(lo, tinv, z), jnp.where(lo, z, tinv)], axis=0)
        uws.append(_dot(tsel.astype(BF16), rhs.astype(BF16)).astype(BF16))
        kdts.append(jnp.concatenate([kf[e] * egl[e] for e in range(2)], axis=0).T.astype(BF16))
    tick()
    r6s = []
    for qk, kdt, uw in zip(qks, kdts, uws):
        zq = jnp.zeros_like(qk)
        zk = jnp.zeros_like(kdt)
        lhs = jnp.concatenate([jnp.where(lo, qk, zq), jnp.where(lo2, kdt, zk),
                               jnp.where(lo, zq, qk), jnp.where(lo2, zk, kdt)], axis=0)
        r6s.append(_dot(lhs, uw))
    while pending:
        tick()
    out = [[None] * A_HEADS for _ in chunks]
    for (g, pp), qh, r6, eg, gls in zip(chains, qhs, r6s, egs, glasts):
        for e, h in enumerate(hs_of(pp)):
            base = e * (L + A_DK)
            o0 = r6[base:base + L, 0:A_DV]
            qkw = r6[base:base + L, A_DV:A_DV + A_DK]
            bm = r6[base + L:base + L + A_DK, 0:A_DV]
            cm = r6[base + L:base + L + A_DK, A_DV:A_DV + A_DK]
            qt = qh[e].astype(F32) * eg[e] - qkw
            out[g][h] = (jnp.concatenate([cm, qt], axis=0).astype(BF16), bm, o0, jnp.exp(gls[e]))
    return out


def _gdn_pipe_kernel(qa_ref, ka_ref, va_ref, aux_ref, sg_ref, s0_ref, ng_ref, oa_ref, sout_ref,
                     s_ref, cq_ref, b_ref, o0_ref, el_ref, *, nsteps):
    i = pl.program_id(0)
    L = CHUNK
    wr = i % 2
    rd = (i + 1) % 2

    @pl.when(i == 0)
    def _():
        s_ref[...] = jnp.zeros(s_ref.shape, F32)
        cq_ref[1] = jnp.zeros(cq_ref.shape[1:], BF16)
        b_ref[1] = jnp.zeros(b_ref.shape[1:], F32)
        o0_ref[1] = jnp.zeros(o0_ref.shape[1:], F32)
        el_ref[1] = jnp.zeros(el_ref.shape[1:], F32)

    j = jnp.maximum(i - 1, 0)
    first = (j % nsteps) == 0
    ng = ng_ref[...]
    state = [jnp.where(first, s0_ref[0, h], s_ref[h]) for h in range(A_HEADS)]

    def recur(g):
        def run():
            rs = [_dot(cq_ref[rd, g, h], state[h].astype(BF16)) for h in range(A_HEADS)]
            for h in range(A_HEADS):
                o = rs[h][A_DK:A_DK + L] + o0_ref[rd, g, h]
                state[h] = state[h] * el_ref[rd, g, h, 0:1, :] - rs[h][0:A_DK] + b_ref[rd, g, h]
                on = o * lax.rsqrt(jnp.mean(o * o, axis=-1, keepdims=True) + EPS) * ng
                sg = sg_ref[0, g * L:(g + 1) * L, h * A_DV:(h + 1) * A_DV].astype(F32)
                oa_ref[0, g * L:(g + 1) * L, h * A_DV:(h + 1) * A_DV] = (on * sg).astype(BF16)
        return run

    chunks = [(qa_ref[0, g * L:(g + 1) * L, :], ka_ref[0, g * L:(g + 1) * L, :], va_ref[0, g * L:(g + 1) * L, :],
               aux_ref[0, g * L:(g + 1) * L, :]) for g in range(GDN_G)]
    prepared = _gdn_prepare(chunks, [recur(g) for g in range(GDN_G)])

    for h in range(A_HEADS):
        s_ref[h] = state[h]

    @pl.when((i >= 1) & ((j % nsteps) == nsteps - 1))
    def _():
        for h in range(A_HEADS):
            sout_ref[0, h] = state[h]

    for g, heads in enumerate(prepared):
        for h in range(A_HEADS):
            cq_ref[wr, g, h] = heads[h][0]
            b_ref[wr, g, h] = heads[h][1]
            o0_ref[wr, g, h] = heads[h][2]
            el_ref[wr, g, h] = jnp.broadcast_to(heads[h][3], (SUBLANES, LANES))


def _gdn_pipe(qa, ka, va, sg, aux, s0, ng):
    bsz, seq, _ = qa.shape
    rows = GDN_G * CHUNK
    assert seq % rows == 0
    nsteps = seq // rows
    total = bsz * nsteps
    cur = lambda w: pl.BlockSpec((1, rows, w), lambda i: (jnp.minimum(i, total - 1) // nsteps,
                                                          jnp.minimum(i, total - 1) % nsteps, 0))
    prev = lambda w: pl.BlockSpec((1, rows, w), lambda i: (jnp.maximum(i - 1, 0) // nsteps,
                                                           jnp.maximum(i - 1, 0) % nsteps, 0))
    st = pl.BlockSpec((1, A_HEADS, A_DK, A_DV), lambda i: (jnp.maximum(i - 1, 0) // nsteps, 0, 0, 0))
    return pl.pallas_call(
        functools.partial(_gdn_pipe_kernel, nsteps=nsteps),
        grid=(total + 1,),
        in_specs=[cur(A_WIDTH), cur(A_WIDTH), cur(A_WIDTH), cur(LANES), prev(A_WIDTH), st, _const_spec((1, A_DV))],
        out_specs=(prev(A_WIDTH), st),
        out_shape=(jax.ShapeDtypeStruct((bsz, seq, A_WIDTH), BF16),
                   jax.ShapeDtypeStruct((bsz, A_HEADS, A_DK, A_DV), F32)),
        scratch_shapes=[pltpu.VMEM((A_HEADS, A_DK, A_DV), F32),
                        pltpu.VMEM((2, GDN_G, A_HEADS, A_DK + CHUNK, A_DV), BF16),
                        pltpu.VMEM((2, GDN_G, A_HEADS, A_DK, A_DV), F32),
                        pltpu.VMEM((2, GDN_G, A_HEADS, CHUNK, A_DV), F32),
                        pltpu.VMEM((2, GDN_G, A_HEADS, SUBLANES, LANES), F32)],
        compiler_params=pltpu.CompilerParams(dimension_semantics=("arbitrary",), vmem_limit_bytes=VMEM_LIMIT),
        name="gdn_pipe",
    )(qa, ka, va, aux, sg, s0, ng)


def _relbias_kernel(rb_ref, o_ref):
    r = lax.broadcasted_iota(jnp.int32, (REL_SIZE, BIAS_EXT), 0)
    p = lax.broadcasted_iota(jnp.int32, (REL_SIZE, BIAS_EXT), 1)
    moff = jnp.where(p < WIN, p, p - BIAS_EXT)
    idx = jnp.clip(BAND_PAST - moff, -(CHUNK - 1), REL_MAX) + (CHUNK - 1)
    base = _dot_hi(rb_ref[...], jnp.where(r == idx, 1.0, 0.0)) * LOG2E
    qi = lax.broadcasted_iota(jnp.int32, (QBLK, WIN), 0) >> CHUNK_SHIFT
    kj = lax.broadcasted_iota(jnp.int32, (QBLK, WIN), 1) >> CHUNK_SHIFT
    in_band = (kj >= qi) & (kj <= qi + BAND_CHUNKS)
    for hh in range(B_HEADS):
        rows = jnp.broadcast_to(base[hh:hh + 1, :], (QBLK, BIAS_EXT))
        toep = pltpu.roll(rows, 0, 1, stride=1, stride_axis=0)
        o_ref[hh] = jnp.where(in_band, toep[:, 0:WIN], MASKED)


def _relbias(rel_bias):
    return pl.pallas_call(
        _relbias_kernel,
        in_specs=[pl.BlockSpec((B_HEADS, REL_SIZE), lambda: (0, 0))],
        out_specs=pl.BlockSpec((B_HEADS, QBLK, WIN), lambda: (0, 0, 0)),
        out_shape=jax.ShapeDtypeStruct((B_HEADS, QBLK, WIN), F32),
        compiler_params=pltpu.CompilerParams(vmem_limit_bytes=VMEM_LIMIT),
        name="relbias",
    )(rel_bias)


def _attn_prompt_kernel(q_ref, k_ref, v_ref, bias_ref, o_ref, kaug_ref, vaug_ref, *, seq):
    c = pl.program_id(1)

    @pl.when(c == 0)
    def _():
        lane = lax.broadcasted_iota(jnp.int32, (seq, B_WIDTH), 1)
        even = ((lane >> B_DH_SHIFT) & 1) == 0
        k = k_ref[0]
        v = v_ref[0]
        for e in range(2):
            mine = even if e == 0 else jnp.logical_not(even)
            kaug_ref[e, 0:BAND_PAST, :] = jnp.zeros((BAND_PAST, B_WIDTH), BF16)
            vaug_ref[e, 0:BAND_PAST, :] = jnp.zeros((BAND_PAST, B_WIDTH), BF16)
            kaug_ref[e, BAND_PAST:BAND_PAST + seq, :] = jnp.where(mine, k, jnp.zeros_like(k))
            vaug_ref[e, BAND_PAST:BAND_PAST + seq, :] = jnp.where(mine, v, jnp.ones_like(v))

    start = pl.multiple_of(c * QBLK, QBLK)
    lane = lax.broadcasted_iota(jnp.int32, (QBLK, LANES), 1)
    def scores(hh):
        lo = (hh // 2) * LANES
        kk = kaug_ref[hh % 2, pl.ds(start, WIN), lo:lo + LANES]
        return _dot_nt(q_ref[0, :, lo:lo + LANES], kk) + bias_ref[hh]

    res = []
    s_next = scores(0)
    for hh in range(B_HEADS):
        s = s_next
        if hh + 1 < B_HEADS:
            s_next = scores(hh + 1)
        lo = (hh // 2) * LANES
        pr = jnp.exp2(s - jnp.max(s, axis=-1, keepdims=True)).astype(BF16)
        vv = vaug_ref[hh % 2, pl.ds(start, WIN), lo:lo + LANES]
        res.append(_dot(pr, vv))
    for pp in range(B_HEADS // 2):
        lo = pp * LANES
        num = jnp.where(lane < B_DH, res[2 * pp], res[2 * pp + 1])
        den = pltpu.roll(jnp.where(lane < B_DH, res[2 * pp + 1], res[2 * pp]), B_DH, 1)
        o_ref[0, :, lo:lo + LANES] = (num / den).astype(BF16)


def _attn_prompt(qb, kb, vb, bias):
    bsz, seq, _ = qb.shape
    assert seq % QBLK == 0
    tok = pl.BlockSpec((1, QBLK, B_WIDTH), lambda b, c: (b, c, 0))
    full = pl.BlockSpec((1, seq, B_WIDTH), lambda b, c: (b, 0, 0))
    bias_spec = pl.BlockSpec((B_HEADS, QBLK, WIN), lambda b, c: (0, 0, 0), pipeline_mode=pl.Buffered(1))
    return pl.pallas_call(
        functools.partial(_attn_prompt_kernel, seq=seq),
        grid=(bsz, seq // QBLK),
        in_specs=[tok, full, full, bias_spec],
        out_specs=tok,
        out_shape=jax.ShapeDtypeStruct((bsz, seq, B_WIDTH), BF16),
        scratch_shapes=[pltpu.VMEM((2, BAND_PAST + seq, B_WIDTH), BF16),
                        pltpu.VMEM((2, BAND_PAST + seq, B_WIDTH), BF16)],
        compiler_params=pltpu.CompilerParams(dimension_semantics=("arbitrary", "arbitrary"),
                                             vmem_limit_bytes=VMEM_LIMIT),
        name="attn_prompt",
    )(qb, kb, vb, bias)


def _attn_sample_kernel(q_ref, kn_ref, vn_ref, kc_ref, vc_ref, bias_ref, o_ref, *, lc, tq):
    kc = kc_ref[0].astype(BF16)
    vc = vc_ref[0].astype(BF16)
    off = BAND_PAST - lc
    lane = lax.broadcasted_iota(jnp.int32, (tq, LANES), 1)
    for pp in range(B_HEADS // 2):
        lo = pp * LANES
        qp = q_ref[0, :, lo:lo + LANES]
        keys = [kc[:, lo:lo + LANES], kn_ref[0, :, lo:lo + LANES]]
        vals = [vc[:, lo:lo + LANES], vn_ref[0, :, lo:lo + LANES]]
        out = jnp.zeros((tq, LANES), F32)
        for e in range(2):
            hh = 2 * pp + e
            mine = (lane >= e * B_DH) & (lane < (e + 1) * B_DH)
            qm = jnp.where(mine, qp, jnp.zeros_like(qp))
            ss = [_dot_nt(qm, keys[0]) + bias_ref[hh, 0:tq, off:off + lc],
                  _dot_nt(qm, keys[1]) + bias_ref[hh, 0:tq, BAND_PAST:BAND_PAST + tq]]
            mx = jnp.maximum(jnp.max(ss[0], axis=-1, keepdims=True), jnp.max(ss[1], axis=-1, keepdims=True))
            es = [jnp.exp2(s - mx) for s in ss]
            den = jnp.sum(es[0], axis=-1, keepdims=True) + jnp.sum(es[1], axis=-1, keepdims=True)
            acc = _dot((es[0] / den).astype(BF16), vals[0]) + _dot((es[1] / den).astype(BF16), vals[1])
            out = out + jnp.where(mine, acc, 0.0)
        o_ref[0, :, lo:lo + LANES] = out.astype(BF16)


def _attn_sample(qb, kb, vb, k_cache, v_cache, bias):
    bsz, tq, _ = qb.shape
    lc = k_cache.shape[1]
    assert lc <= BAND_PAST and tq <= CHUNK
    tok = pl.BlockSpec((1, tq, B_WIDTH), lambda b: (b, 0, 0))
    cache = pl.BlockSpec((1, lc, B_WIDTH), lambda b: (b, 0, 0))
    bias_spec = pl.BlockSpec((B_HEADS, QBLK, WIN), lambda b: (0, 0, 0), pipeline_mode=pl.Buffered(1))
    return pl.pallas_call(
        functools.partial(_attn_sample_kernel, lc=lc, tq=tq),
        grid=(bsz,),
        in_specs=[tok, tok, tok, cache, cache, bias_spec],
        out_specs=tok,
        out_shape=jax.ShapeDtypeStruct((bsz, tq, B_WIDTH), BF16),
        compiler_params=pltpu.CompilerParams(dimension_semantics=("arbitrary",), vmem_limit_bytes=VMEM_LIMIT),
        name="attn_sample",
    )(qb, kb, vb, k_cache.reshape(bsz, lc, B_WIDTH), v_cache.reshape(bsz, lc, B_WIDTH), bias)


def _outffn_kernel(x_ref, oa_ref, ob_ref, mod_ref, g2_ref, woa_ref, wob_ref, wup_ref, wdn_ref, y_ref, *, nb, tt):
    n = nb * tt
    mod = mod_ref[...]
    gate1 = mod[:, 2:3, :]
    sh2 = mod[:, 3:4, :]
    sc2 = mod[:, 4:5, :]
    gate2 = mod[:, 5:6, :]
    mix = _dot(oa_ref[...].reshape(n, A_WIDTH), woa_ref[...]) + _dot(ob_ref[...].reshape(n, B_WIDTH), wob_ref[...])
    y1 = x_ref[...] + gate1 * mix.reshape(nb, tt, D_MODEL)
    y1f = y1.reshape(n, D_MODEL)
    ms = jnp.mean(y1f * y1f, axis=-1, keepdims=True)
    yn = y1f * lax.rsqrt(ms + EPS) * g2_ref[...]
    h2 = (yn.reshape(nb, tt, D_MODEL) * (1.0 + sc2) + sh2).reshape(n, D_MODEL).astype(BF16)
    acc = jnp.zeros((n, D_MODEL), F32)
    fc = D_MODEL
    for jj in range(D_FF // fc):
        u = _dot(h2, wup_ref[:, jj * fc:(jj + 1) * fc])
        r = jnp.maximum(u, 0.0)
        acc = acc + _dot((r * r).astype(BF16), wdn_ref[jj * fc:(jj + 1) * fc, :])
    y_ref[...] = y1 + gate2 * acc.reshape(nb, tt, D_MODEL)


def _outffn(x, oa, ob, mod, wts, *, nb, tt):
    bsz, seq, _ = x.shape
    assert seq % tt == 0 and bsz % nb == 0
    tok = lambda w: pl.BlockSpec((nb, tt, w), lambda b, t: (b, t, 0))
    once = lambda shape: pl.BlockSpec(shape, lambda b, t: (0,) * len(shape), pipeline_mode=pl.Buffered(1))
    return pl.pallas_call(
        functools.partial(_outffn_kernel, nb=nb, tt=tt),
        grid=(bsz // nb, seq // tt),
        in_specs=[tok(D_MODEL), tok(A_WIDTH), tok(B_WIDTH),
                  pl.BlockSpec((nb, 6, D_MODEL), lambda b, t: (b, 0, 0)), once((1, D_MODEL)),
                  once((A_WIDTH, D_MODEL)), once((B_WIDTH, D_MODEL)), once((D_MODEL, D_FF)), once((D_FF, D_MODEL))],
        out_specs=tok(D_MODEL),
        out_shape=jax.ShapeDtypeStruct((bsz, seq, D_MODEL), F32),
        compiler_params=pltpu.CompilerParams(dimension_semantics=("arbitrary", "arbitrary"),
                                             vmem_limit_bytes=VMEM_LIMIT),
        name="outffn",
    )(x, oa, ob, mod, wts["g2"], wts["woa"], wts["wob"], wts["wup"], wts["wdn"])


def _prep_weights(norm1_g, norm2_g, w_in, conv_w, a_log, dt_bias, gdn_norm_g, qn_g, kn_g, w_out, w_up, w_down):
    pad_ba = jnp.zeros((D_MODEL, LANES - 2 * A_HEADS), F32)
    pad_row = lambda v: jnp.zeros((1, LANES), F32).at[0, A_HEADS:2 * A_HEADS].set(v)
    head = jnp.arange(B_WIDTH) // B_DH
    return dict(
        g1=norm1_g.reshape(1, D_MODEL), g2=norm2_g.reshape(1, D_MODEL),
        wqkv=w_in[:, QA:GA].astype(BF16), wgate=w_in[:, GA:BA].astype(BF16),
        wba=jnp.concatenate([w_in[:, BA:QB], pad_ba], axis=1).astype(BF16),
        wqb=w_in[:, QB:KB].astype(BF16), wkb=w_in[:, KB:VB].astype(BF16), wvb=w_in[:, VB:IN_COLS].astype(BF16),
        cw=conv_w, alog=pad_row(a_log), dtb=pad_row(dt_bias), ng=gdn_norm_g.reshape(1, A_DV),
        qng=jnp.tile(qn_g, B_HEADS).reshape(1, B_WIDTH), kng=jnp.tile(kn_g, B_HEADS).reshape(1, B_WIDTH),
        seg=(head[:, None] == head[None, :]).astype(BF16),
        woa=w_out[:A_WIDTH].astype(BF16), wob=w_out[A_WIDTH:].astype(BF16),
        wup=w_up.astype(BF16), wdn=w_down.astype(BF16))


def _layer(x, mod, left, s0, cache, bias, wts, *, nb, tt, chunk):
    bsz, seq, _ = x.shape
    qa, ka, va, sg, aux, qb, kb, vb, kb32, vb32, conv_state = _inproj(x, mod, left, wts, nb=nb, tt=tt)
    if chunk == CHUNK and seq % (GDN_G * CHUNK) == 0:
        oa, s_new = _gdn_pipe(qa, ka, va, sg, aux, s0, wts["ng"])
    else:
        oa, s_new = _gdn(qa, ka, va, sg, aux, s0, wts["ng"], L=chunk)
    if cache is None:
        ob = _attn_prompt(qb, kb, vb, bias)
    else:
        ob = _attn_sample(qb, kb, vb, cache[0], cache[1], bias)
    y = _outffn(x, oa, ob, mod, wts, nb=nb, tt=tt)
    keep = kb32.shape[1]
    return (y, conv_state, s_new, kb32.reshape(bsz, keep, B_HEADS, B_DH), vb32.reshape(bsz, keep, B_HEADS, B_DH))


def kernel(x_prompt, x_sample, state_conv, state_gdn, cache_k_band, cache_v_band, c_prompt, c_sample,
           w_mod, b_mod, norm1_g, norm2_g, w_in, conv_w, a_log, dt_bias, gdn_norm_g, qn_g, kn_g,
           rel_bias, w_out, w_up, w_down):
    depth = w_mod.shape[0]
    bp, tp, _ = x_prompt.shape
    bs, ts, _ = x_sample.shape
    yp, ys = x_prompt, x_sample
    c_all = jnp.concatenate([c_prompt, c_sample], axis=0)
    outs = [[] for _ in range(8)]
    for l in range(depth):
        wts = _prep_weights(norm1_g[l], norm2_g[l], w_in[l], conv_w[l], a_log[l], dt_bias[l], gdn_norm_g[l],
                            qn_g[l], kn_g[l], w_out[l], w_up[l], w_down[l])
        mod = _modulation(c_all, w_mod[l], b_mod[l]).reshape(bp + bs, 6, D_MODEL)
        bias = _relbias(rel_bias[l])
        yp, cp, gp, kp, vp = _layer(
            yp, mod[:bp], jnp.zeros((bp, CONV_W - 1, CONV_CH), F32), jnp.zeros((bp, A_HEADS, A_DK, A_DV), F32),
            None, bias, wts, nb=1, tt=min(512, tp), chunk=min(CHUNK, tp))
        ys, cs, gs, ks, vs = _layer(
            ys, mod[bp:], state_conv[l], state_gdn[l], (cache_k_band[l], cache_v_band[l]), bias, wts,
            nb=bs, tt=ts, chunk=min(CHUNK, ts))
        for acc, val in zip(outs, (cp, gp, kp, vp, cs, gs, ks, vs)):
            acc.append(val)
    return (yp, ys) + tuple(jnp.stack(o) for o in outs)
```
